```python
import math
import jax, jax.numpy as jnp
from jax import lax
import numpy as np

D_MODEL = 1024
BATCH = 16
SEQ = 2048
DEPTH = 1

E_A = D_MODEL
CONV_K = 31
N_GROUPS_A = 8
E_B = D_MODEL
CHUNK = 128
N_HEADS_B = 8
HEAD_DIM_B = E_B // N_HEADS_B
N_BRANCH = 2
COLS_A_IN = 2 * E_A
COLS_A_GATE = E_A
COLS_B_UV = 2 * E_B
COLS_B_GATE = E_B
COLS_MERGE = N_BRANCH * D_MODEL
D_IN = COLS_A_IN + COLS_A_GATE + COLS_B_UV + COLS_B_GATE + COLS_MERGE
DEEPNORM_ALPHA = (2.0 * DEPTH) ** 0.25
DEEPNORM_BETA = (8.0 * DEPTH) ** -0.25
LN_EPS = 1e-5

kernel_name = "hybrid_conformer_conv_chunked_gmlp_gated_deepnorm"


def _layer_norm(x, g, b):
    xf = x.astype(jnp.float32)
    mu = jnp.mean(xf, axis=-1, keepdims=True)
    var = jnp.mean(jnp.square(xf - mu), axis=-1, keepdims=True)
    y = (xf - mu) * lax.rsqrt(var + LN_EPS)
    return (y * g.astype(jnp.float32) + b.astype(jnp.float32)).astype(x.dtype)


def _group_norm_channels(x, g, b, n_groups):
    shp = x.shape
    xf = x.astype(jnp.float32).reshape(shp[:-1] + (n_groups, shp[-1] // n_groups))
    mu = jnp.mean(xf, axis=-1, keepdims=True)
    var = jnp.mean(jnp.square(xf - mu), axis=-1, keepdims=True)
    y = ((xf - mu) * lax.rsqrt(var + LN_EPS)).reshape(shp)
    return (y * g.astype(jnp.float32) + b.astype(jnp.float32)).astype(x.dtype)


def _conformer_conv_branch(a_in, a_gate, conv_w, conv_b, gn_g, gn_b, w_pa):
    val, gate = jnp.split(a_in, 2, axis=-1)
    h = val * jax.nn.sigmoid(gate)
    h = lax.conv_general_dilated(
        h, conv_w[:, None, :].astype(h.dtype),
        window_strides=(1,), padding=[(CONV_K - 1, 0)],
        dimension_numbers=("NWC", "WIO", "NWC"),
        feature_group_count=E_A) + conv_b
    h = _group_norm_channels(h, gn_g, gn_b, N_GROUPS_A)
    h = jax.nn.silu(h) * jax.nn.silu(a_gate)
    return jnp.einsum("bse,ed->bsd", h, w_pa)


def _chunked_gmlp_branch(b_uv, b_gate, ln_v_g, ln_v_b, w_spatial, b_spatial, w_pb):
    bsz, seq = b_uv.shape[0], b_uv.shape[1]
    z = jax.nn.gelu(b_uv)
    u, v = jnp.split(z, 2, axis=-1)
    v = _layer_norm(v, ln_v_g, ln_v_b)
    v = v.reshape(bsz, seq // CHUNK, CHUNK, N_HEADS_B, HEAD_DIM_B)
    causal = jnp.tril(jnp.ones((CHUNK, CHUNK), dtype=bool))
    ws = jnp.where(causal[None], w_spatial, jnp.zeros((), w_spatial.dtype))
    v_mix = jnp.einsum("hts,bnshd->bnthd", ws, v) + b_spatial.T[None, None, :, :, None]
    s = u * v_mix.reshape(bsz, seq, E_B)
    s = s * jax.nn.silu(b_gate)
    return jnp.einsum("bse,ed->bsd", s, w_pb)


def setup_inputs(seed: int = 0) -> dict:
    key = jax.random.key(seed)
    ks = jax.random.split(key, 20)
    f32 = jnp.float32
    nrm = lambda k, shp, sc: jax.random.normal(k, shp, f32) * sc
    return {
        "x": jax.random.normal(ks[0], (BATCH, SEQ, D_MODEL), f32),
        "w_in": nrm(ks[1], (D_MODEL, D_IN), D_MODEL ** -0.5),
        "b_in": nrm(ks[2], (D_IN,), 0.02),
        "conv_w": nrm(ks[3], (CONV_K, E_A), CONV_K ** -0.5),
        "conv_b": nrm(ks[4], (E_A,), 0.02),
        "gn_g": 1.0 + nrm(ks[5], (E_A,), 0.02),
        "gn_b": nrm(ks[6], (E_A,), 0.02),
        "ln_v_g": 1.0 + nrm(ks[7], (E_B,), 0.02),
        "ln_v_b": nrm(ks[8], (E_B,), 0.02),
        "w_spatial": nrm(ks[9], (N_HEADS_B, CHUNK, CHUNK), CHUNK ** -0.5),
        "b_spatial": 1.0 + nrm(ks[10], (N_HEADS_B, CHUNK), 0.1),
        "w_pa": nrm(ks[11], (E_A, D_MODEL), E_A ** -0.5),
        "w_pb": nrm(ks[12], (E_B, D_MODEL), E_B ** -0.5),
        "w_o": nrm(ks[13], (D_MODEL, D_MODEL), DEEPNORM_BETA * D_MODEL ** -0.5),
        "b_o": nrm(ks[14], (D_MODEL,), 0.02),
        "ln_out_g": 1.0 + nrm(ks[15], (D_MODEL,), 0.02),
        "ln_out_b": nrm(ks[16], (D_MODEL,), 0.02),
    }


def reference(x, w_in, b_in, conv_w, conv_b, gn_g, gn_b, ln_v_g, ln_v_b,
              w_spatial, b_spatial, w_pa, w_pb, w_o, b_o, ln_out_g, ln_out_b):
    splits = np.cumsum([COLS_A_IN, COLS_A_GATE, COLS_B_UV, COLS_B_GATE]).tolist()
    for _ in range(DEPTH):
        proj = jnp.einsum("bsd,dk->bsk", x, w_in) + b_in
        a_in, a_gate, b_uv, b_gate, merge = jnp.split(proj, splits, axis=-1)
        y_a = _conformer_conv_branch(a_in, a_gate, conv_w, conv_b, gn_g, gn_b, w_pa)
        y_b = _chunked_gmlp_branch(b_uv, b_gate, ln_v_g, ln_v_b, w_spatial, b_spatial, w_pb)
        g_a, g_b = jnp.split(jax.nn.sigmoid(merge), 2, axis=-1)
        mixed = g_a * y_a + g_b * y_b
        sub = jnp.einsum("bsd,de->bse", mixed, w_o) + b_o
        x = _layer_norm(DEEPNORM_ALPHA * x + sub, ln_out_g, ln_out_b)
    return x
```

```python
import jax
import jax.numpy as jnp
from jax import lax
from jax.experimental import pallas as pl
from jax.experimental.pallas import tpu as pltpu

D_MODEL = 1024
CONV_K = 31
CHUNK = 128
N_HEADS = 8
N_GROUPS = 8
LANES = 128
SUBLANES = 8
MXU_N = 256
LN_EPS = 1e-5
DEEPNORM_ALPHA = 2.0 ** 0.25

OFF_A_VAL = 0
OFF_A_GLU = D_MODEL
OFF_A_GATE = 2 * D_MODEL
OFF_B_U = 3 * D_MODEL
OFF_B_V = 4 * D_MODEL
OFF_B_GATE = 5 * D_MODEL
OFF_MERGE_A = 6 * D_MODEL
OFF_MERGE_B = 7 * D_MODEL
D_IN = 8 * D_MODEL

TS = 256
HALO = 32
CONV_ROWS = 32
LN_ROWS = 16
N_SLAB = D_MODEL // LANES
N_PAIR = D_MODEL // MXU_N
VMEM_LIMIT_BYTES = 52 * 1024 * 1024

assert D_MODEL // N_GROUPS == LANES and D_MODEL // N_HEADS == LANES
assert TS % CHUNK == 0 and TS // CHUNK == 2 and HALO >= CONV_K - 1


def _dot(a, b):
    return jnp.dot(a, b, preferred_element_type=jnp.float32)


def _body(x_ref, w_in_ref, b_in_ref, cw_ref, cb_ref, gng_ref, gnb_ref, lvg_ref, lvb_ref,
          wsp_ref, bsp_ref, wpa_ref, wpb_ref, wo_ref, bo_ref, log_ref, lob_ref,
          o_ref,
          xb_ref, hbuf_ref, gate_ref, abuf_ref, ubuf_ref, vbuf_ref, vn_ref, sbuf_ref,
          mix_ref, wtril_ref):
    b_idx = pl.program_id(0)
    s_idx = pl.program_id(1)

    @pl.when((b_idx == 0) & (s_idx == 0))
    def _mask_spatial_weights():
        row = lax.broadcasted_iota(jnp.int32, (CHUNK, CHUNK), 0)
        col = lax.broadcasted_iota(jnp.int32, (CHUNK, CHUNK), 1)
        for h in range(N_HEADS):
            wtril_ref[h] = jnp.where(row >= col, wsp_ref[h], 0.0).astype(jnp.bfloat16)

    @pl.when(s_idx == 0)
    def _zero_conv_history():
        hbuf_ref[:, 0:HALO, :] = jnp.zeros((N_SLAB, HALO, LANES), jnp.float32)

    xb_ref[...] = x_ref[0].astype(jnp.bfloat16)

    def proj(off, p):
        lo = off + MXU_N * p
        return _dot(xb_ref[...], w_in_ref[:, lo:lo + MXU_N]) + b_in_ref[:, lo:lo + MXU_N]

    for p in range(N_PAIR):
        h = proj(OFF_A_VAL, p) * jax.nn.sigmoid(proj(OFF_A_GLU, p))
        hbuf_ref[2 * p, HALO:HALO + TS, :] = h[:, :LANES]
        hbuf_ref[2 * p + 1, HALO:HALO + TS, :] = h[:, LANES:]
        gate_ref[...] = jax.nn.silu(proj(OFF_A_GATE, p))

        for jj in range(2):
            j = 2 * p + jj
            lanes = slice(LANES * j, LANES * (j + 1))
            wk = [jnp.broadcast_to(cw_ref[k:k + 1, lanes], (SUBLANES, LANES)) for k in range(CONV_K)]
            cb = cb_ref[:, lanes]
            gg = gng_ref[:, lanes]
            gb = gnb_ref[:, lanes]

            def conv_rows(i, carry, j=j, jj=jj, lanes=lanes, wk=wk, cb=cb, gg=gg, gb=gb):
                base = pl.multiple_of(i * CONV_ROWS, CONV_ROWS)
                accs = []
                for r in range(CONV_ROWS // SUBLANES):
                    acc = None
                    for k in range(CONV_K):
                        start = base + (SUBLANES * r + HALO - (CONV_K - 1) + k)
                        t = wk[k] * hbuf_ref[j, pl.ds(start, SUBLANES), :]
                        acc = t if acc is None else acc + t
                    accs.append(acc)
                c = jnp.concatenate(accs, axis=0) + cb
                mu = jnp.mean(c, axis=-1, keepdims=True)
                d = c - mu
                var = jnp.mean(d * d, axis=-1, keepdims=True)
                gn = d * lax.rsqrt(var + LN_EPS) * gg + gb
                g = gate_ref[pl.ds(base, CONV_ROWS), LANES * jj:LANES * (jj + 1)]
                abuf_ref[pl.ds(base, CONV_ROWS), lanes] = (jax.nn.silu(gn) * g).astype(jnp.bfloat16)
                return carry

            lax.fori_loop(0, TS // CONV_ROWS, conv_rows, 0)

    for j in range(N_SLAB):
        hbuf_ref[j, 0:HALO, :] = hbuf_ref[j, TS:TS + HALO, :]

    for p in range(N_PAIR):
        cols = slice(MXU_N * p, MXU_N * (p + 1))
        ubuf_ref[:, cols] = jax.nn.gelu(proj(OFF_B_U, p))
        vbuf_ref[:, cols] = jax.nn.gelu(proj(OFF_B_V, p))

    lvg = lvg_ref[...]
    lvb = lvb_ref[...]

    def ln_v_rows(i, carry):
        r0 = pl.multiple_of(i * LN_ROWS, LN_ROWS)
        v = vbuf_ref[pl.ds(r0, LN_ROWS), :]
        mu = jnp.mean(v, axis=-1, keepdims=True)
        d = v - mu
        var = jnp.mean(d * d, axis=-1, keepdims=True)
        vn_ref[pl.ds(r0, LN_ROWS), :] = (d * lax.rsqrt(var + LN_EPS) * lvg + lvb).astype(jnp.bfloat16)
        return carry

    lax.fori_loop(0, TS // LN_ROWS, ln_v_rows, 0)

    for p in range(N_PAIR):
        bgate = jax.nn.silu(proj(OFF_B_GATE, p))
        for jj in range(2):
            hd = 2 * p + jj
            lanes = slice(LANES * hd, LANES * (hd + 1))
            rhs = jnp.concatenate([vn_ref[0:CHUNK, lanes], vn_ref[CHUNK:2 * CHUNK, lanes]], axis=1)
            vmix = _dot(wtril_ref[hd], rhs)
            bsp = bsp_ref[hd]
            for n in range(2):
                rows = slice(CHUNK * n, CHUNK * (n + 1))
                vm = vmix[:, LANES * n:LANES * (n + 1)] + bsp
                sblk = ubuf_ref[rows, lanes] * vm * bgate[rows, LANES * jj:LANES * (jj + 1)]
                sbuf_ref[rows, lanes] = sblk.astype(jnp.bfloat16)

    for p in range(N_PAIR):
        cols = slice(MXU_N * p, MXU_N * (p + 1))
        ya = _dot(abuf_ref[...], wpa_ref[:, cols])
        yb = _dot(sbuf_ref[...], wpb_ref[:, cols])
        ga = jax.nn.sigmoid(proj(OFF_MERGE_A, p))
        gb2 = jax.nn.sigmoid(proj(OFF_MERGE_B, p))
        mix_ref[:, cols] = (ga * ya + gb2 * yb).astype(jnp.bfloat16)

    for p in range(N_PAIR):
        cols = slice(MXU_N * p, MXU_N * (p + 1))
        sub = _dot(mix_ref[...], wo_ref[:, cols]) + bo_ref[:, cols]
        o_ref[0, :, cols] = DEEPNORM_ALPHA * x_ref[0, :, cols] + sub

    log = log_ref[...]
    lob = lob_ref[...]

    def ln_out_rows(i, carry):
        r0 = pl.multiple_of(i * LN_ROWS, LN_ROWS)
        y = o_ref[0, pl.ds(r0, LN_ROWS), :]
        mu = jnp.mean(y, axis=-1, keepdims=True)
        d = y - mu
        var = jnp.mean(d * d, axis=-1, keepdims=True)
        o_ref[0, pl.ds(r0, LN_ROWS), :] = d * lax.rsqrt(var + LN_EPS) * log + lob
        return carry

    lax.fori_loop(0, TS // LN_ROWS, ln_out_rows, 0)


def _const_spec(shape):
    zeros = (0,) * len(shape)
    return pl.BlockSpec(shape, lambda b, s: zeros, pipeline_mode=pl.Buffered(1))


def kernel(x, w_in, b_in, conv_w, conv_b, gn_g, gn_b, ln_v_g, ln_v_b, w_spatial, b_spatial, w_pa, w_pb, w_o, b_o, ln_out_g, ln_out_b):
    B, S, D = x.shape
    assert D == D_MODEL and S % TS == 0 and w_in.shape == (D_MODEL, D_IN)
    bf16 = jnp.bfloat16
    f32 = jnp.float32
    row = lambda v: v.reshape(1, -1).astype(f32)

    operands = [
        x,
        w_in.astype(bf16), row(b_in),
        conv_w.astype(f32), row(conv_b), row(gn_g), row(gn_b), row(ln_v_g), row(ln_v_b),
        w_spatial.astype(f32), b_spatial.astype(f32).reshape(N_HEADS, CHUNK, 1),
        w_pa.astype(bf16), w_pb.astype(bf16), w_o.astype(bf16), row(b_o),
        row(ln_out_g), row(ln_out_b),
    ]
    in_specs = [pl.BlockSpec((1, TS, D), lambda b, s: (b, s, 0))]
    in_specs += [_const_spec(op.shape) for op in operands[1:]]

    scratch = [
        pltpu.VMEM((TS, D), bf16),
        pltpu.VMEM((N_SLAB, HALO + TS, LANES), f32),
        pltpu.VMEM((TS, MXU_N), f32),
        pltpu.VMEM((TS, D), bf16),
        pltpu.VMEM((TS, D), f32),
        pltpu.VMEM((TS, D), f32),
        pltpu.VMEM((TS, D), bf16),
        pltpu.VMEM((TS, D), bf16),
        pltpu.VMEM((TS, D), bf16),
        pltpu.VMEM((N_HEADS, CHUNK, CHUNK), bf16),
    ]
    return pl.pallas_call(
        _body,
        grid=(B, S // TS),
        in_specs=in_specs,
        out_specs=pl.BlockSpec((1, TS, D), lambda b, s: (b, s, 0)),
        out_shape=jax.ShapeDtypeStruct((B, S, D), x.dtype),
        scratch_shapes=scratch,
        compiler_params=pltpu.CompilerParams(
            dimension_semantics=("arbitrary", "arbitrary"),
            vmem_limit_bytes=VMEM_LIMIT_BYTES,
        ),
        name="fused_hybrid_layer",
    )(*operands)
```

```python
import jax
import jax.numpy as jnp
from jax import lax
from jax.experimental import pallas as pl
from jax.experimental.pallas import tpu as pltpu

D_MODEL = 1024
CONV_K = 31
CHUNK = 128
N_HEADS = 8
N_GROUPS = 8
LANES = 128
SUBLANES = 8
MXU_N = 256
LN_EPS = 1e-5
DEEPNORM_ALPHA = 2.0 ** 0.25

N_PAIR = D_MODEL // MXU_N
BLK_A_VAL = 0 * N_PAIR
BLK_A_GLU = 1 * N_PAIR
BLK_A_GATE = 2 * N_PAIR
BLK_B_U = 3 * N_PAIR
BLK_B_V = 4 * N_PAIR
BLK_B_GATE = 5 * N_PAIR
BLK_MERGE_A = 6 * N_PAIR
BLK_MERGE_B = 7 * N_PAIR
N_BLK_IN = 8 * N_PAIR

TS = 256
HALO = 32
N_SLAB = D_MODEL // LANES
VMEM_LIMIT_BYTES = 52 * 1024 * 1024

assert D_MODEL // N_GROUPS == LANES and D_MODEL // N_HEADS == LANES
assert TS == 2 * CHUNK and HALO >= CONV_K - 1 and HALO % SUBLANES == 0


def _dot(a, b):
    return jnp.dot(a, b, preferred_element_type=jnp.float32)


def _normalise(y, gain, bias):
    mu = jnp.mean(y, axis=-1, keepdims=True)
    d = y - mu
    var = jnp.mean(d * d, axis=-1, keepdims=True)
    return d * lax.rsqrt(var + LN_EPS) * gain + bias


def _body(x_ref, w_in_ref, b_in_ref, cw_ref, cb_ref, gng_ref, gnb_ref, lvg_ref, lvb_ref,
          wsp_ref, bsp_ref, wpa_ref, wpb_ref, wo_ref, bo_ref, log_ref, lob_ref,
          o_ref,
          xb_ref, hbuf_ref, gate_ref, abuf_ref, ubuf_ref, vbuf_ref, vn_ref, sbuf_ref,
          mix_ref, wtril_ref):
    b_idx = pl.program_id(0)
    s_idx = pl.program_id(1)

    @pl.when((b_idx == 0) & (s_idx == 0))
    def _mask_spatial_weights():
        row = lax.broadcasted_iota(jnp.int32, (CHUNK, CHUNK), 0)
        col = lax.broadcasted_iota(jnp.int32, (CHUNK, CHUNK), 1)
        for h in range(N_HEADS):
            wtril_ref[h] = jnp.where(row >= col, wsp_ref[h], 0.0).astype(jnp.bfloat16)

    @pl.when(s_idx == 0)
    def _zero_conv_history():
        hbuf_ref[:, 0:HALO, :] = jnp.zeros((N_SLAB, HALO, LANES), jnp.float32)

    xb_ref[...] = x_ref[0].astype(jnp.bfloat16)

    def proj(blk, p):
        cols = slice(MXU_N * (blk + p), MXU_N * (blk + p + 1))
        return _dot(xb_ref[...], w_in_ref[blk + p]) + b_in_ref[:, cols]

    def conv_slab(j):
        lanes = slice(LANES * j, LANES * (j + 1))
        wk = [jnp.broadcast_to(cw_ref[k:k + 1, lanes], (SUBLANES, LANES)) for k in range(CONV_K)]
        accs = []
        for r in range(TS // SUBLANES):
            acc = None
            for k in range(CONV_K):
                start = SUBLANES * r + HALO - (CONV_K - 1) + k
                t = wk[k] * hbuf_ref[j, start:start + SUBLANES, :]
                acc = t if acc is None else acc + t
            accs.append(acc)
        c = jnp.concatenate(accs, axis=0) + cb_ref[:, lanes]
        gn = _normalise(c, gng_ref[:, lanes], gnb_ref[:, lanes])
        abuf_ref[:, lanes] = (jax.nn.silu(gn) * gate_ref[:, lanes]).astype(jnp.bfloat16)

    for p in range(N_PAIR):
        cols = slice(MXU_N * p, MXU_N * (p + 1))
        h = proj(BLK_A_VAL, p) * jax.nn.sigmoid(proj(BLK_A_GLU, p))
        hbuf_ref[2 * p, HALO:HALO + TS, :] = h[:, :LANES]
        hbuf_ref[2 * p + 1, HALO:HALO + TS, :] = h[:, LANES:]
        gate_ref[:, cols] = jax.nn.silu(proj(BLK_A_GATE, p))
        conv_slab(2 * p)
        conv_slab(2 * p + 1)

    for j in range(N_SLAB):
        hbuf_ref[j, 0:HALO, :] = hbuf_ref[j, TS:TS + HALO, :]

    for p in range(N_PAIR):
        cols = slice(MXU_N * p, MXU_N * (p + 1))
        ubuf_ref[:, cols] = jax.nn.gelu(proj(BLK_B_U, p))
        vbuf_ref[:, cols] = jax.nn.gelu(proj(BLK_B_V, p))

    vn_ref[...] = _normalise(vbuf_ref[...], lvg_ref[...], lvb_ref[...]).astype(jnp.bfloat16)

    for p in range(N_PAIR):
        bgate = jax.nn.silu(proj(BLK_B_GATE, p))
        for jj in range(2):
            hd = 2 * p + jj
            lanes = slice(LANES * hd, LANES * (hd + 1))
            rhs = jnp.concatenate([vn_ref[0:CHUNK, lanes], vn_ref[CHUNK:2 * CHUNK, lanes]], axis=1)
            vmix = _dot(wtril_ref[hd], rhs)
            bsp = bsp_ref[hd]
            for n in range(2):
                rows = slice(CHUNK * n, CHUNK * (n + 1))
                vm = vmix[:, LANES * n:LANES * (n + 1)] + bsp
                sblk = ubuf_ref[rows, lanes] * vm * bgate[rows, LANES * jj:LANES * (jj + 1)]
                sbuf_ref[rows, lanes] = sblk.astype(jnp.bfloat16)

    for p in range(N_PAIR):
        cols = slice(MXU_N * p, MXU_N * (p + 1))
        ya = _dot(abuf_ref[...], wpa_ref[p])
        yb = _dot(sbuf_ref[...], wpb_ref[p])
        ga = jax.nn.sigmoid(proj(BLK_MERGE_A, p))
        gb = jax.nn.sigmoid(proj(BLK_MERGE_B, p))
        mix_ref[:, cols] = (ga * ya + gb * yb).astype(jnp.bfloat16)

    for p in range(N_PAIR):
        cols = slice(MXU_N * p, MXU_N * (p + 1))
        sub = _dot(mix_ref[...], wo_ref[p]) + bo_ref[:, cols]
        o_ref[0, :, cols] = DEEPNORM_ALPHA * x_ref[0, :, cols] + sub

    o_ref[0] = _normalise(o_ref[0], log_ref[...], lob_ref[...])


def _const_spec(shape):
    zeros = (0,) * len(shape)
    return pl.BlockSpec(shape, lambda b, s: zeros, pipeline_mode=pl.Buffered(1))


def _col_blocks(w):
    k, n = w.shape
    return w.astype(jnp.bfloat16).reshape(k, n // MXU_N, MXU_N).transpose(1, 0, 2)


def kernel(x, w_in, b_in, conv_w, conv_b, gn_g, gn_b, ln_v_g, ln_v_b, w_spatial, b_spatial, w_pa, w_pb, w_o, b_o, ln_out_g, ln_out_b):
    B, S, D = x.shape
    assert D == D_MODEL and S % TS == 0 and w_in.shape == (D_MODEL, N_BLK_IN * MXU_N)
    bf16 = jnp.bfloat16
    f32 = jnp.float32
    row = lambda v: v.reshape(1, -1).astype(f32)

    operands = [
        x,
        _col_blocks(w_in), row(b_in),
        conv_w.astype(f32), row(conv_b), row(gn_g), row(gn_b), row(ln_v_g), row(ln_v_b),
        w_spatial.astype(f32), b_spatial.astype(f32).reshape(N_HEADS, CHUNK, 1),
        _col_blocks(w_pa), _col_blocks(w_pb), _col_blocks(w_o), row(b_o),
        row(ln_out_g), row(ln_out_b),
    ]
    in_specs = [pl.BlockSpec((1, TS, D), lambda b, s: (b, s, 0))]
    in_specs += [_const_spec(op.shape) for op in operands[1:]]

    scratch = [
        pltpu.VMEM((TS, D), bf16),
        pltpu.VMEM((N_SLAB, HALO + TS, LANES), f32),
        pltpu.VMEM((TS, D), f32),
        pltpu.VMEM((TS, D), bf16),
        pltpu.VMEM((TS, D), f32),
        pltpu.VMEM((TS, D), f32),
        pltpu.VMEM((TS, D), bf16),
        pltpu.VMEM((TS, D), bf16),
        pltpu.VMEM((TS, D), bf16),
        pltpu.VMEM((N_HEADS, CHUNK, CHUNK), bf16),
    ]
    return pl.pallas_call(
        _body,
        grid=(B, S // TS),
        in_specs=in_specs,
        out_specs=pl.BlockSpec((1, TS, D), lambda b, s: (b, s, 0)),
        out_shape=jax.ShapeDtypeStruct((B, S, D), x.dtype),
        scratch_shapes=scratch,
        compiler_params=pltpu.CompilerParams(
            dimension_semantics=("arbitrary", "arbitrary"),
            vmem_limit_bytes=VMEM_LIMIT_BYTES,
        ),
        name="fused_hybrid_layer",
    )(*operands)
```

```python
import functools

import jax
import jax.numpy as jnp
from jax import lax
from jax.experimental import pallas as pl
from jax.experimental.pallas import tpu as pltpu

D_MODEL = 1024
CONV_K = 31
CHUNK = 128
N_HEADS = 8
N_GROUPS = 8
LANES = 128
SUBLANES = 8
MXU_N = 256
LN_EPS = 1e-5
DEEPNORM_ALPHA = 2.0 ** 0.25

N_PAIR = D_MODEL // MXU_N
BLK_A_VAL = 0 * N_PAIR
BLK_A_GLU = 1 * N_PAIR
BLK_A_GATE = 2 * N_PAIR
BLK_B_U = 3 * N_PAIR
BLK_B_V = 4 * N_PAIR
BLK_B_GATE = 5 * N_PAIR
BLK_MERGE_A = 6 * N_PAIR
BLK_MERGE_B = 7 * N_PAIR
N_BLK_IN = 8 * N_PAIR
BLK_W_PA = N_BLK_IN
BLK_W_PB = N_BLK_IN + N_PAIR
BLK_W_O = N_BLK_IN + 2 * N_PAIR
_LOG2_E = 1.4426950408889634
GELU_K1 = -2.0 * (2.0 / 3.141592653589793) ** 0.5 * _LOG2_E
GELU_K3 = GELU_K1 * 0.044715

TS = 256
HALO = 32
N_SLAB = D_MODEL // LANES
VMEM_LIMIT_BYTES = 52 * 1024 * 1024

assert D_MODEL // N_GROUPS == LANES and D_MODEL // N_HEADS == LANES
assert TS == 2 * CHUNK and HALO >= CONV_K - 1 and HALO % SUBLANES == 0


def _dot(a, b):
    return jnp.dot(a, b, preferred_element_type=jnp.float32)


def _gelu_tanh(x):
    return x / (1.0 + jnp.exp2(x * (GELU_K1 + GELU_K3 * (x * x))))


def _normalise(y, gain, bias):
    mu = jnp.mean(y, axis=-1, keepdims=True)
    d = y - mu
    var = jnp.mean(d * d, axis=-1, keepdims=True)
    return d * lax.rsqrt(var + LN_EPS) * gain + bias


def _body(tiles_per_seq, n_tiles,
          x_ref, w_ref, b_in_ref, cw_ref, cb_ref, gng_ref, gnb_ref, lvg_ref, lvb_ref,
          wsp_ref, bsp_ref, bo_ref, log_ref, lob_ref,
          o_ref,
          xb_ref, hbuf_ref, gate_ref, abuf_ref, ubuf_ref, vbuf_ref, vn_ref, sbuf_ref,
          mix_ref, wtril_ref, xprev_ref, xbprev_ref):
    step = pl.program_id(0)
    tile = jnp.minimum(step, n_tiles - 1)

    @pl.when(step == 0)
    def _first_step():
        row = lax.broadcasted_iota(jnp.int32, (CHUNK, CHUNK), 0)
        col = lax.broadcasted_iota(jnp.int32, (CHUNK, CHUNK), 1)
        for h in range(N_HEADS):
            wtril_ref[h] = jnp.where(row >= col, wsp_ref[h], 0.0).astype(jnp.bfloat16)
        abuf_ref[...] = jnp.zeros(abuf_ref.shape, abuf_ref.dtype)
        sbuf_ref[...] = jnp.zeros(sbuf_ref.shape, sbuf_ref.dtype)
        xprev_ref[...] = jnp.zeros(xprev_ref.shape, xprev_ref.dtype)
        xbprev_ref[...] = jnp.zeros(xbprev_ref.shape, xbprev_ref.dtype)

    @pl.when(lax.rem(tile, tiles_per_seq) == 0)
    def _zero_conv_history():
        hbuf_ref[:, 0:HALO, :] = jnp.zeros((N_SLAB, HALO, LANES), jnp.float32)

    def proj(lhs_ref, blk, p):
        cols = slice(MXU_N * (blk + p), MXU_N * (blk + p + 1))
        return _dot(lhs_ref[...], w_ref[blk + p]) + b_in_ref[:, cols]

    def merge(p):
        cols = slice(MXU_N * p, MXU_N * (p + 1))
        ya = _dot(abuf_ref[...], w_ref[BLK_W_PA + p])
        yb = _dot(sbuf_ref[...], w_ref[BLK_W_PB + p])
        ga = jax.nn.sigmoid(proj(xbprev_ref, BLK_MERGE_A, p))
        gb = jax.nn.sigmoid(proj(xbprev_ref, BLK_MERGE_B, p))
        mix_ref[:, cols] = (ga * ya + gb * yb).astype(jnp.bfloat16)

    def out_proj():
        for p in range(N_PAIR):
            cols = slice(MXU_N * p, MXU_N * (p + 1))
            sub = _dot(mix_ref[...], w_ref[BLK_W_O + p]) + bo_ref[:, cols]
            o_ref[0, :, cols] = DEEPNORM_ALPHA * xprev_ref[:, cols] + sub
        o_ref[0] = _normalise(o_ref[0], log_ref[...], lob_ref[...])

    def branch_a_proj(p):
        cols = slice(MXU_N * p, MXU_N * (p + 1))
        h = proj(xb_ref, BLK_A_VAL, p) * jax.nn.sigmoid(proj(xb_ref, BLK_A_GLU, p))
        hbuf_ref[2 * p, HALO:HALO + TS, :] = h[:, :LANES]
        hbuf_ref[2 * p + 1, HALO:HALO + TS, :] = h[:, LANES:]
        gate_ref[:, cols] = jax.nn.silu(proj(xb_ref, BLK_A_GATE, p))

    def conv_slab(j):
        lanes = slice(LANES * j, LANES * (j + 1))
        wk = [jnp.broadcast_to(cw_ref[k:k + 1, lanes], (SUBLANES, LANES)) for k in range(CONV_K)]
        accs = []
        for r in range(TS // SUBLANES):
            acc = None
            for k in range(CONV_K):
                start = SUBLANES * r + HALO - (CONV_K - 1) + k
                t = wk[k] * hbuf_ref[j, start:start + SUBLANES, :]
                acc = t if acc is None else acc + t
            accs.append(acc)
        c = jnp.concatenate(accs, axis=0) + cb_ref[:, lanes]
        gn = _normalise(c, gng_ref[:, lanes], gnb_ref[:, lanes])
        abuf_ref[:, lanes] = (jax.nn.silu(gn) * gate_ref[:, lanes]).astype(jnp.bfloat16)

    def branch_b():
        for p in range(N_PAIR):
            cols = slice(MXU_N * p, MXU_N * (p + 1))
            ubuf_ref[:, cols] = _gelu_tanh(proj(xb_ref, BLK_B_U, p))
            vbuf_ref[:, cols] = _gelu_tanh(proj(xb_ref, BLK_B_V, p))
        vn_ref[...] = _normalise(vbuf_ref[...], lvg_ref[...], lvb_ref[...]).astype(jnp.bfloat16)
        for p in range(N_PAIR):
            gate_ref[:, MXU_N * p:MXU_N * (p + 1)] = jax.nn.silu(proj(xb_ref, BLK_B_GATE, p))
            for jj in range(2):
                hd = 2 * p + jj
                lanes = slice(LANES * hd, LANES * (hd + 1))
                rhs = jnp.concatenate([vn_ref[0:CHUNK, lanes], vn_ref[CHUNK:2 * CHUNK, lanes]], axis=1)
                vmix = _dot(wtril_ref[hd], rhs)
                bsp = bsp_ref[hd]
                for n in range(2):
                    rows = slice(CHUNK * n, CHUNK * (n + 1))
                    vm = vmix[:, LANES * n:LANES * (n + 1)] + bsp
                    sblk = ubuf_ref[rows, lanes] * vm * gate_ref[rows, lanes]
                    sbuf_ref[rows, lanes] = sblk.astype(jnp.bfloat16)

    xb_ref[...] = x_ref[0].astype(jnp.bfloat16)
    for p in range(N_PAIR):
        branch_a_proj(p)
    for p in range(N_PAIR):
        merge(p)
    out_proj()
    for j in range(N_SLAB):
        conv_slab(j)
    for j in range(N_SLAB):
        hbuf_ref[j, 0:HALO, :] = hbuf_ref[j, TS:TS + HALO, :]
    branch_b()
    xprev_ref[...] = x_ref[0]
    xbprev_ref[...] = xb_ref[...]


def _const_spec(shape):
    zeros = (0,) * len(shape)
    return pl.BlockSpec(shape, lambda i: zeros, pipeline_mode=pl.Buffered(1))


def _col_blocks(w):
    k, n = w.shape
    return w.astype(jnp.bfloat16).reshape(k, n // MXU_N, MXU_N).transpose(1, 0, 2)


def kernel(x, w_in, b_in, conv_w, conv_b, gn_g, gn_b, ln_v_g, ln_v_b, w_spatial, b_spatial, w_pa, w_pb, w_o, b_o, ln_out_g, ln_out_b):
    B, S, D = x.shape
    assert D == D_MODEL and S % TS == 0 and w_in.shape == (D_MODEL, N_BLK_IN * MXU_N)
    bf16 = jnp.bfloat16
    f32 = jnp.float32
    row = lambda v: v.reshape(1, -1).astype(f32)
    tiles_per_seq = S // TS
    n_tiles = B * tiles_per_seq

    operands = [
        x,
        _col_blocks(jnp.concatenate([w_in, w_pa, w_pb, w_o], axis=1)), row(b_in),
        conv_w.astype(f32), row(conv_b), row(gn_g), row(gn_b), row(ln_v_g), row(ln_v_b),
        w_spatial.astype(f32), b_spatial.astype(f32).reshape(N_HEADS, CHUNK, 1),
        row(b_o), row(ln_out_g), row(ln_out_b),
    ]

    def tile_block(t):
        return (t // tiles_per_seq, t % tiles_per_seq, 0)

    in_specs = [pl.BlockSpec((1, TS, D), lambda i: tile_block(jnp.minimum(i, n_tiles - 1)))]
    in_specs += [_const_spec(op.shape) for op in operands[1:]]

    scratch = [
        pltpu.VMEM((TS, D), bf16),
        pltpu.VMEM((N_SLAB, HALO + TS, LANES), f32),
        pltpu.VMEM((TS, D), f32),
        pltpu.VMEM((TS, D), bf16),
        pltpu.VMEM((TS, D), f32),
        pltpu.VMEM((TS, D), f32),
        pltpu.VMEM((TS, D), bf16),
        pltpu.VMEM((TS, D), bf16),
        pltpu.VMEM((TS, D), bf16),
        pltpu.VMEM((N_HEADS, CHUNK, CHUNK), bf16),
        pltpu.VMEM((TS, D), f32),
        pltpu.VMEM((TS, D), bf16),
    ]
    return pl.pallas_call(
        functools.partial(_body, tiles_per_seq, n_tiles),
        grid=(n_tiles + 1,),
        in_specs=in_specs,
        out_specs=pl.BlockSpec((1, TS, D), lambda i: tile_block(jnp.maximum(i - 1, 0))),
        out_shape=jax.ShapeDtypeStruct((B, S, D), x.dtype),
        scratch_shapes=scratch,
        compiler_params=pltpu.CompilerParams(
            dimension_semantics=("arbitrary",),
            vmem_limit_bytes=VMEM_LIMIT_BYTES,
        ),
        name="fused_hybrid_layer",
    )(*operands)
```

```python
import functools

import jax
import jax.numpy as jnp
from jax import lax
from jax.experimental import pallas as pl
from jax.experimental.pallas import tpu as pltpu

D_MODEL = 1024
CONV_K = 31
CHUNK = 128
N_HEADS = 8
N_GROUPS = 8
LANES = 128
SUBLANES = 8
MXU_N = 256
LN_EPS = 1e-5
DEEPNORM_ALPHA = 2.0 ** 0.25

N_PAIR = D_MODEL // MXU_N
BLK_A_VAL = 0 * N_PAIR
BLK_A_GLU = 1 * N_PAIR
BLK_A_GATE = 2 * N_PAIR
BLK_B_U = 3 * N_PAIR
BLK_B_V = 4 * N_PAIR
BLK_B_GATE = 5 * N_PAIR
BLK_MERGE_A = 6 * N_PAIR
BLK_MERGE_B = 7 * N_PAIR
N_BLK_IN = 8 * N_PAIR
BLK_W_PA = N_BLK_IN
BLK_W_PB = N_BLK_IN + N_PAIR
BLK_W_O = N_BLK_IN + 2 * N_PAIR
N_BLK_W = N_BLK_IN + 3 * N_PAIR
_LOG2_E = 1.4426950408889634
GELU_K1 = -2.0 * (2.0 / 3.141592653589793) ** 0.5 * _LOG2_E
GELU_K3 = GELU_K1 * 0.044715

TS = 256
HALO = 32
N_SLAB = D_MODEL // LANES
VMEM_LIMIT_BYTES = 52 * 1024 * 1024

assert D_MODEL // N_GROUPS == LANES and D_MODEL // N_HEADS == LANES
assert TS == 2 * CHUNK and HALO >= CONV_K - 1 and HALO % SUBLANES == 0


def _dot(a, b):
    return jnp.dot(a, b, preferred_element_type=jnp.float32)


def _gelu_tanh(x):
    return x / (1.0 + jnp.exp2(x * (GELU_K1 + GELU_K3 * (x * x))))


def _normalise(y, gain, bias):
    mu = jnp.mean(y, axis=-1, keepdims=True)
    d = y - mu
    var = jnp.mean(d * d, axis=-1, keepdims=True)
    return d * lax.rsqrt(var + LN_EPS) * gain + bias


def _load_weight(src_hbm_ref, first_blk, n_blk, w_ref, stage_ref, sem_ref):
    def block_copy(c, slot):
        cols = pl.ds(pl.multiple_of(c * MXU_N, MXU_N), MXU_N)
        return pltpu.make_async_copy(src_hbm_ref.at[:, cols], stage_ref.at[slot], sem_ref.at[slot])

    block_copy(0, 0).start()

    def cast_block(c, carry):
        slot = lax.rem(c, 2)
        block_copy(c, slot).wait()

        @pl.when(c + 1 < n_blk)
        def _():
            block_copy(c + 1, 1 - slot).start()

        w_ref[first_blk + c] = stage_ref[slot].astype(jnp.bfloat16)
        return carry

    lax.fori_loop(0, n_blk, cast_block, 0)


def _body(tiles_per_seq, n_tiles,
          x_ref, w_in_hbm_ref, b_in_ref, cw_ref, cb_ref, gng_ref, gnb_ref, lvg_ref, lvb_ref,
          wsp_ref, bsp_ref, wpa_hbm_ref, wpb_hbm_ref, wo_hbm_ref, bo_ref, log_ref, lob_ref,
          o_ref,
          xb_ref, hbuf_ref, gate_ref, abuf_ref, ubuf_ref, vbuf_ref, vn_ref, sbuf_ref,
          mix_ref, wtril_ref, xprev_ref, xbprev_ref, w_ref, wstage_ref, wsem_ref):
    step = pl.program_id(0)
    tile = jnp.minimum(step, n_tiles - 1)

    @pl.when(step == 0)
    def _first_step():
        _load_weight(w_in_hbm_ref, 0, N_BLK_IN, w_ref, wstage_ref, wsem_ref)
        _load_weight(wpa_hbm_ref, BLK_W_PA, N_PAIR, w_ref, wstage_ref, wsem_ref)
        _load_weight(wpb_hbm_ref, BLK_W_PB, N_PAIR, w_ref, wstage_ref, wsem_ref)
        _load_weight(wo_hbm_ref, BLK_W_O, N_PAIR, w_ref, wstage_ref, wsem_ref)
        row = lax.broadcasted_iota(jnp.int32, (CHUNK, CHUNK), 0)
        col = lax.broadcasted_iota(jnp.int32, (CHUNK, CHUNK), 1)
        for h in range(N_HEADS):
            wtril_ref[h] = jnp.where(row >= col, wsp_ref[h], 0.0).astype(jnp.bfloat16)
        abuf_ref[...] = jnp.zeros(abuf_ref.shape, abuf_ref.dtype)
        sbuf_ref[...] = jnp.zeros(sbuf_ref.shape, sbuf_ref.dtype)
        xprev_ref[...] = jnp.zeros(xprev_ref.shape, xprev_ref.dtype)
        xbprev_ref[...] = jnp.zeros(xbprev_ref.shape, xbprev_ref.dtype)

    @pl.when(lax.rem(tile, tiles_per_seq) == 0)
    def _zero_conv_history():
        hbuf_ref[:, 0:HALO, :] = jnp.zeros((N_SLAB, HALO, LANES), jnp.float32)

    def proj(lhs_ref, blk, p):
        cols = slice(MXU_N * (blk + p), MXU_N * (blk + p + 1))
        return _dot(lhs_ref[...], w_ref[blk + p]) + b_in_ref[:, cols]

    def merge(p):
        cols = slice(MXU_N * p, MXU_N * (p + 1))
        ya = _dot(abuf_ref[...], w_ref[BLK_W_PA + p])
        yb = _dot(sbuf_ref[...], w_ref[BLK_W_PB + p])
        ga = jax.nn.sigmoid(proj(xbprev_ref, BLK_MERGE_A, p))
        gb = jax.nn.sigmoid(proj(xbprev_ref, BLK_MERGE_B, p))
        mix_ref[:, cols] = (ga * ya + gb * yb).astype(jnp.bfloat16)

    def out_proj():
        for p in range(N_PAIR):
            cols = slice(MXU_N * p, MXU_N * (p + 1))
            sub = _dot(mix_ref[...], w_ref[BLK_W_O + p]) + bo_ref[:, cols]
            o_ref[0, :, cols] = DEEPNORM_ALPHA * xprev_ref[:, cols] + sub
        o_ref[0] = _normalise(o_ref[0], log_ref[...], lob_ref[...])

    def branch_a_proj(p):
        cols = slice(MXU_N * p, MXU_N * (p + 1))
        h = proj(xb_ref, BLK_A_VAL, p) * jax.nn.sigmoid(proj(xb_ref, BLK_A_GLU, p))
        hbuf_ref[2 * p, HALO:HALO + TS, :] = h[:, :LANES]
        hbuf_ref[2 * p + 1, HALO:HALO + TS, :] = h[:, LANES:]
        gate_ref[:, cols] = jax.nn.silu(proj(xb_ref, BLK_A_GATE, p))

    def conv_slab(j):
        lanes = slice(LANES * j, LANES * (j + 1))
        wk = [jnp.broadcast_to(cw_ref[k:k + 1, lanes], (SUBLANES, LANES)) for k in range(CONV_K)]
        accs = []
        for r in range(TS // SUBLANES):
            acc = None
            for k in range(CONV_K):
                start = SUBLANES * r + HALO - (CONV_K - 1) + k
                t = wk[k] * hbuf_ref[j, start:start + SUBLANES, :]
                acc = t if acc is None else acc + t
            accs.append(acc)
        c = jnp.concatenate(accs, axis=0) + cb_ref[:, lanes]
        gn = _normalise(c, gng_ref[:, lanes], gnb_ref[:, lanes])
        abuf_ref[:, lanes] = (jax.nn.silu(gn) * gate_ref[:, lanes]).astype(jnp.bfloat16)

    def branch_b():
        for p in range(N_PAIR):
            cols = slice(MXU_N * p, MXU_N * (p + 1))
            ubuf_ref[:, cols] = _gelu_tanh(proj(xb_ref, BLK_B_U, p))
            vbuf_ref[:, cols] = _gelu_tanh(proj(xb_ref, BLK_B_V, p))
        vn_ref[...] = _normalise(vbuf_ref[...], lvg_ref[...], lvb_ref[...]).astype(jnp.bfloat16)
        for p in range(N_PAIR):
            gate_ref[:, MXU_N * p:MXU_N * (p + 1)] = jax.nn.silu(proj(xb_ref, BLK_B_GATE, p))
            for jj in range(2):
                hd = 2 * p + jj
                lanes = slice(LANES * hd, LANES * (hd + 1))
                rhs = jnp.concatenate([vn_ref[0:CHUNK, lanes], vn_ref[CHUNK:2 * CHUNK, lanes]], axis=1)
                vmix = _dot(wtril_ref[hd], rhs)
                bsp = bsp_ref[hd]
                for n in range(2):
                    rows = slice(CHUNK * n, CHUNK * (n + 1))
                    vm = vmix[:, LANES * n:LANES * (n + 1)] + bsp
                    sblk = ubuf_ref[rows, lanes] * vm * gate_ref[rows, lanes]
                    sbuf_ref[rows, lanes] = sblk.astype(jnp.bfloat16)

    xb_ref[...] = x_ref[0].astype(jnp.bfloat16)
    for p in range(N_PAIR):
        branch_a_proj(p)
    for p in range(N_PAIR):
        merge(p)
    out_proj()
    for j in range(N_SLAB):
        conv_slab(j)
    for j in range(N_SLAB):
        hbuf_ref[j, 0:HALO, :] = hbuf_ref[j, TS:TS + HALO, :]
    branch_b()
    xprev_ref[...] = x_ref[0]
    xbprev_ref[...] = xb_ref[...]


def _const_spec(shape):
    zeros = (0,) * len(shape)
    return pl.BlockSpec(shape, lambda i: zeros, pipeline_mode=pl.Buffered(1))


def kernel(x, w_in, b_in, conv_w, conv_b, gn_g, gn_b, ln_v_g, ln_v_b, w_spatial, b_spatial, w_pa, w_pb, w_o, b_o, ln_out_g, ln_out_b):
    B, S, D = x.shape
    assert D == D_MODEL and S % TS == 0 and w_in.shape == (D_MODEL, N_BLK_IN * MXU_N)
    bf16 = jnp.bfloat16
    f32 = jnp.float32
    row = lambda v: v.reshape(1, -1).astype(f32)
    tiles_per_seq = S // TS
    n_tiles = B * tiles_per_seq

    weights = (w_in, w_pa, w_pb, w_o)
    assert all(w.dtype == f32 and w.shape[0] == D_MODEL for w in weights)
    operands = [
        x,
        w_in, row(b_in),
        conv_w.astype(f32), row(conv_b), row(gn_g), row(gn_b), row(ln_v_g), row(ln_v_b),
        w_spatial.astype(f32), b_spatial.astype(f32).reshape(N_HEADS, CHUNK, 1),
        w_pa, w_pb, w_o, row(b_o),
        row(ln_out_g), row(ln_out_b),
    ]

    def tile_block(t):
        return (t // tiles_per_seq, t % tiles_per_seq, 0)

    in_specs = [pl.BlockSpec((1, TS, D), lambda i: tile_block(jnp.minimum(i, n_tiles - 1)))]
    in_specs += [pl.BlockSpec(memory_space=pl.ANY) if any(op is w for w in weights) else _const_spec(op.shape)
                 for op in operands[1:]]

    scratch = [
        pltpu.VMEM((TS, D), bf16),
        pltpu.VMEM((N_SLAB, HALO + TS, LANES), f32),
        pltpu.VMEM((TS, D), f32),
        pltpu.VMEM((TS, D), bf16),
        pltpu.VMEM((TS, D), f32),
        pltpu.VMEM((TS, D), f32),
        pltpu.VMEM((TS, D), bf16),
        pltpu.VMEM((TS, D), bf16),
        pltpu.VMEM((TS, D), bf16),
        pltpu.VMEM((N_HEADS, CHUNK, CHUNK), bf16),
        pltpu.VMEM((TS, D), f32),
        pltpu.VMEM((TS, D), bf16),
        pltpu.VMEM((N_BLK_W, D_MODEL, MXU_N), bf16),
        pltpu.VMEM((2, D_MODEL, MXU_N), f32),
        pltpu.SemaphoreType.DMA((2,)),
    ]
    return pl.pallas_call(
        functools.partial(_body, tiles_per_seq, n_tiles),
        grid=(n_tiles + 1,),
        in_specs=in_specs,
        out_specs=pl.BlockSpec((1, TS, D), lambda i: tile_block(jnp.maximum(i - 1, 0))),
        out_shape=jax.ShapeDtypeStruct((B, S, D), x.dtype),
        scratch_shapes=scratch,
        compiler_params=pltpu.CompilerParams(
            dimension_semantics=("arbitrary",),
            vmem_limit_bytes=VMEM_LIMIT_BYTES,
        ),
        name="fused_hybrid_layer",
    )(*operands)
```

```python
import functools

import jax
import jax.numpy as jnp
from jax import lax
from jax.experimental import pallas as pl
from jax.experimental.pallas import tpu as pltpu

D_MODEL = 1024
CONV_K = 31
CHUNK = 128
N_HEADS = 8
N_GROUPS = 8
LANES = 128
SUBLANES = 8
MXU_N = 256
LN_EPS = 1e-5
DEEPNORM_ALPHA = 2.0 ** 0.25

N_PAIR = D_MODEL // MXU_N
BLK_A_VAL = 0 * N_PAIR
BLK_A_GLU = 1 * N_PAIR
BLK_A_GATE = 2 * N_PAIR
BLK_B_U = 3 * N_PAIR
BLK_B_V = 4 * N_PAIR
BLK_B_GATE = 5 * N_PAIR
BLK_MERGE_A = 6 * N_PAIR
BLK_MERGE_B = 7 * N_PAIR
N_BLK_IN = 8 * N_PAIR
BLK_W_PA = N_BLK_IN
BLK_W_PB = N_BLK_IN + N_PAIR
BLK_W_O = N_BLK_IN + 2 * N_PAIR
N_BLK_W = N_BLK_IN + 3 * N_PAIR
N_WSTAGE = 4
_LOG2_E = 1.4426950408889634
GELU_K1 = -2.0 * (2.0 / 3.141592653589793) ** 0.5 * _LOG2_E
GELU_K3 = GELU_K1 * 0.044715

TS = 512
HALO = 32
N_SLAB = D_MODEL // LANES
VMEM_LIMIT_BYTES = 58 * 1024 * 1024

assert D_MODEL // N_GROUPS == LANES and D_MODEL // N_HEADS == LANES
assert TS % (2 * CHUNK) == 0 and HALO >= CONV_K - 1 and HALO % SUBLANES == 0


def _dot(a, b):
    return jnp.dot(a, b, preferred_element_type=jnp.float32)


def _gelu_tanh(x):
    return x / (1.0 + jnp.exp2(x * (GELU_K1 + GELU_K3 * (x * x))))


def _normalise(y, gain, bias):
    mu = jnp.mean(y, axis=-1, keepdims=True)
    d = y - mu
    var = jnp.mean(d * d, axis=-1, keepdims=True)
    return d * lax.rsqrt(var + LN_EPS) * gain + bias


def _load_weight(src_hbm_ref, first_blk, n_blk, w_ref, stage_ref, sem_ref):
    ahead = N_WSTAGE - 1
    assert n_blk >= ahead

    def block_copy(c):
        c = jnp.asarray(c, jnp.int32)
        slot = lax.rem(c, N_WSTAGE)
        cols = pl.ds(pl.multiple_of(c * MXU_N, MXU_N), MXU_N)
        return pltpu.make_async_copy(src_hbm_ref.at[:, cols], stage_ref.at[slot], sem_ref.at[slot])

    for c in range(ahead):
        block_copy(c).start()

    def cast_block(c, carry):
        block_copy(c).wait()

        @pl.when(c + ahead < n_blk)
        def _():
            block_copy(c + ahead).start()

        w_ref[first_blk + c] = stage_ref[lax.rem(c, N_WSTAGE)].astype(jnp.bfloat16)
        return carry

    lax.fori_loop(0, n_blk, cast_block, 0)


def _body(tiles_per_seq, n_tiles,
          x_ref, w_in_hbm_ref, b_in_ref, cw_ref, cb_ref, gng_ref, gnb_ref, lvg_ref, lvb_ref,
          wsp_ref, bsp_ref, wpa_hbm_ref, wpb_hbm_ref, wo_hbm_ref, bo_ref, log_ref, lob_ref,
          o_ref,
          xb_ref, hbuf_ref, gate_ref, abuf_ref, ubuf_ref, vbuf_ref, vn_ref, sbuf_ref,
          mix_ref, wtril_ref, xprev_ref, xbprev_ref, w_ref, wstage_ref, wsem_ref):
    step = pl.program_id(0)
    tile = jnp.minimum(step, n_tiles - 1)

    @pl.when(step == 0)
    def _first_step():
        _load_weight(w_in_hbm_ref, 0, N_BLK_IN, w_ref, wstage_ref, wsem_ref)
        _load_weight(wpa_hbm_ref, BLK_W_PA, N_PAIR, w_ref, wstage_ref, wsem_ref)
        _load_weight(wpb_hbm_ref, BLK_W_PB, N_PAIR, w_ref, wstage_ref, wsem_ref)
        _load_weight(wo_hbm_ref, BLK_W_O, N_PAIR, w_ref, wstage_ref, wsem_ref)
        row = lax.broadcasted_iota(jnp.int32, (CHUNK, CHUNK), 0)
        col = lax.broadcasted_iota(jnp.int32, (CHUNK, CHUNK), 1)
        for h in range(N_HEADS):
            wtril_ref[h] = jnp.where(row >= col, wsp_ref[h], 0.0).astype(jnp.bfloat16)
        abuf_ref[...] = jnp.zeros(abuf_ref.shape, abuf_ref.dtype)
        sbuf_ref[...] = jnp.zeros(sbuf_ref.shape, sbuf_ref.dtype)
        xprev_ref[...] = jnp.zeros(xprev_ref.shape, xprev_ref.dtype)
        xbprev_ref[...] = jnp.zeros(xbprev_ref.shape, xbprev_ref.dtype)

    @pl.when(lax.rem(tile, tiles_per_seq) == 0)
    def _zero_conv_history():
        hbuf_ref[:, 0:HALO, :] = jnp.zeros((N_SLAB, HALO, LANES), jnp.float32)

    def proj(lhs_ref, blk, p):
        cols = slice(MXU_N * (blk + p), MXU_N * (blk + p + 1))
        return _dot(lhs_ref[...], w_ref[blk + p]) + b_in_ref[:, cols]

    def merge(p):
        cols = slice(MXU_N * p, MXU_N * (p + 1))
        ya = _dot(abuf_ref[...], w_ref[BLK_W_PA + p])
        yb = _dot(sbuf_ref[...], w_ref[BLK_W_PB + p])
        ga = jax.nn.sigmoid(proj(xbprev_ref, BLK_MERGE_A, p))
        gb = jax.nn.sigmoid(proj(xbprev_ref, BLK_MERGE_B, p))
        mix_ref[:, cols] = (ga * ya + gb * yb).astype(jnp.bfloat16)

    def out_proj():
        for p in range(N_PAIR):
            cols = slice(MXU_N * p, MXU_N * (p + 1))
            sub = _dot(mix_ref[...], w_ref[BLK_W_O + p]) + bo_ref[:, cols]
            o_ref[0, :, cols] = DEEPNORM_ALPHA * xprev_ref[:, cols] + sub
        o_ref[0] = _normalise(o_ref[0], log_ref[...], lob_ref[...])

    def branch_a_proj(p):
        cols = slice(MXU_N * p, MXU_N * (p + 1))
        h = proj(xb_ref, BLK_A_VAL, p) * jax.nn.sigmoid(proj(xb_ref, BLK_A_GLU, p))
        hbuf_ref[2 * p, HALO:HALO + TS, :] = h[:, :LANES]
        hbuf_ref[2 * p + 1, HALO:HALO + TS, :] = h[:, LANES:]
        gate_ref[:, cols] = jax.nn.silu(proj(xb_ref, BLK_A_GATE, p))

    def conv_slab(j):
        lanes = slice(LANES * j, LANES * (j + 1))
        wk = [jnp.broadcast_to(cw_ref[k:k + 1, lanes], (SUBLANES, LANES)) for k in range(CONV_K)]
        accs = []
        for r in range(TS // SUBLANES):
            acc = None
            for k in range(CONV_K):
                start = SUBLANES * r + HALO - (CONV_K - 1) + k
                t = wk[k] * hbuf_ref[j, start:start + SUBLANES, :]
                acc = t if acc is None else acc + t
            accs.append(acc)
        c = jnp.concatenate(accs, axis=0) + cb_ref[:, lanes]
        gn = _normalise(c, gng_ref[:, lanes], gnb_ref[:, lanes])
        abuf_ref[:, lanes] = (jax.nn.silu(gn) * gate_ref[:, lanes]).astype(jnp.bfloat16)

    def branch_b():
        for p in range(N_PAIR):
            cols = slice(MXU_N * p, MXU_N * (p + 1))
            ubuf_ref[:, cols] = _gelu_tanh(proj(xb_ref, BLK_B_U, p))
            vbuf_ref[:, cols] = _gelu_tanh(proj(xb_ref, BLK_B_V, p))
        vn_ref[...] = _normalise(vbuf_ref[...], lvg_ref[...], lvb_ref[...]).astype(jnp.bfloat16)
        for p in range(N_PAIR):
            gate_ref[:, MXU_N * p:MXU_N * (p + 1)] = jax.nn.silu(proj(xb_ref, BLK_B_GATE, p))
            for jj in range(2):
                hd = 2 * p + jj
                lanes = slice(LANES * hd, LANES * (hd + 1))
                bsp = bsp_ref[hd]
                for c0 in range(0, TS // CHUNK, 2):
                    rhs = jnp.concatenate([vn_ref[CHUNK * c0:CHUNK * (c0 + 1), lanes],
                                           vn_ref[CHUNK * (c0 + 1):CHUNK * (c0 + 2), lanes]], axis=1)
                    vmix = _dot(wtril_ref[hd], rhs)
                    for n in range(2):
                        rows = slice(CHUNK * (c0 + n), CHUNK * (c0 + n + 1))
                        vm = vmix[:, LANES * n:LANES * (n + 1)] + bsp
                        sblk = ubuf_ref[rows, lanes] * vm * gate_ref[rows, lanes]
                        sbuf_ref[rows, lanes] = sblk.astype(jnp.bfloat16)

    xb_ref[...] = x_ref[0].astype(jnp.bfloat16)
    for p in range(N_PAIR):
        branch_a_proj(p)
    for p in range(N_PAIR):
        merge(p)
    out_proj()
    for j in range(N_SLAB):
        conv_slab(j)
    for j in range(N_SLAB):
        hbuf_ref[j, 0:HALO, :] = hbuf_ref[j, TS:TS + HALO, :]
    branch_b()
    xprev_ref[...] = x_ref[0]
    xbprev_ref[...] = xb_ref[...]


def _const_spec(shape):
    zeros = (0,) * len(shape)
    return pl.BlockSpec(shape, lambda i: zeros, pipeline_mode=pl.Buffered(1))


def kernel(x, w_in, b_in, conv_w, conv_b, gn_g, gn_b, ln_v_g, ln_v_b, w_spatial, b_spatial, w_pa, w_pb, w_o, b_o, ln_out_g, ln_out_b):
    B, S, D = x.shape
    assert D == D_MODEL and S % TS == 0 and w_in.shape == (D_MODEL, N_BLK_IN * MXU_N)
    bf16 = jnp.bfloat16
    f32 = jnp.float32
    row = lambda v: v.reshape(1, -1).astype(f32)
    tiles_per_seq = S // TS
    n_tiles = B * tiles_per_seq

    weights = (w_in, w_pa, w_pb, w_o)
    assert all(w.dtype == f32 and w.shape[0] == D_MODEL for w in weights)
    operands = [
        x,
        w_in, row(b_in),
        conv_w.astype(f32), row(conv_b), row(gn_g), row(gn_b), row(ln_v_g), row(ln_v_b),
        w_spatial.astype(f32), b_spatial.astype(f32).reshape(N_HEADS, CHUNK, 1),
        w_pa, w_pb, w_o, row(b_o),
        row(ln_out_g), row(ln_out_b),
    ]

    def tile_block(t):
        return (t // tiles_per_seq, t % tiles_per_seq, 0)

    in_specs = [pl.BlockSpec((1, TS, D), lambda i: tile_block(jnp.minimum(i, n_tiles - 1)))]
    in_specs += [pl.BlockSpec(memory_space=pl.ANY) if any(op is w for w in weights) else _const_spec(op.shape)
                 for op in operands[1:]]

    scratch = [
        pltpu.VMEM((TS, D), bf16),
        pltpu.VMEM((N_SLAB, HALO + TS, LANES), f32),
        pltpu.VMEM((TS, D), f32),
        pltpu.VMEM((TS, D), bf16),
        pltpu.VMEM((TS, D), f32),
        pltpu.VMEM((TS, D), f32),
        pltpu.VMEM((TS, D), bf16),
        pltpu.VMEM((TS, D), bf16),
        pltpu.VMEM((TS, D), bf16),
        pltpu.VMEM((N_HEADS, CHUNK, CHUNK), bf16),
        pltpu.VMEM((TS, D), f32),
        pltpu.VMEM((TS, D), bf16),
        pltpu.VMEM((N_BLK_W, D_MODEL, MXU_N), bf16),
        pltpu.VMEM((N_WSTAGE, D_MODEL, MXU_N), f32),
        pltpu.SemaphoreType.DMA((N_WSTAGE,)),
    ]
    return pl.pallas_call(
        functools.partial(_body, tiles_per_seq, n_tiles),
        grid=(n_tiles + 1,),
        in_specs=in_specs,
        out_specs=pl.BlockSpec((1, TS, D), lambda i: tile_block(jnp.maximum(i - 1, 0))),
        out_shape=jax.ShapeDtypeStruct((B, S, D), x.dtype),
        scratch_shapes=scratch,
        compiler_params=pltpu.CompilerParams(
            dimension_semantics=("arbitrary",),
            vmem_limit_bytes=VMEM_LIMIT_BYTES,
        ),
        name="fused_hybrid_layer",
    )(*operands)
```

```python
import functools

import jax
import jax.numpy as jnp
from jax import lax
from jax.experimental import pallas as pl
from jax.experimental.pallas import tpu as pltpu

D_MODEL = 1024
CONV_K = 31
CHUNK = 128
N_HEADS = 8
N_GROUPS = 8
LANES = 128
SUBLANES = 8
MXU_N = 256
LN_EPS = 1e-5
DEEPNORM_ALPHA = 2.0 ** 0.25

N_PAIR = D_MODEL // MXU_N
BLK_A_VAL = 0 * N_PAIR
BLK_A_GLU = 1 * N_PAIR
BLK_A_GATE = 2 * N_PAIR
BLK_B_U = 3 * N_PAIR
BLK_B_V = 4 * N_PAIR
BLK_B_GATE = 5 * N_PAIR
BLK_MERGE_A = 6 * N_PAIR
BLK_MERGE_B = 7 * N_PAIR
N_BLK_IN = 8 * N_PAIR
BLK_W_PA = N_BLK_IN
BLK_W_PB = N_BLK_IN + N_PAIR
BLK_W_O = N_BLK_IN + 2 * N_PAIR
N_BLK_W = N_BLK_IN + 3 * N_PAIR
N_WSTAGE = 4
_LOG2_E = 1.4426950408889634
GELU_K1 = -2.0 * (2.0 / 3.141592653589793) ** 0.5 * _LOG2_E
GELU_K3 = GELU_K1 * 0.044715

TS = 256
HALO = 32
N_SLAB = D_MODEL // LANES
VMEM_LIMIT_BYTES = 58 * 1024 * 1024

assert D_MODEL // N_GROUPS == LANES and D_MODEL // N_HEADS == LANES
assert TS % (2 * CHUNK) == 0 and HALO >= CONV_K - 1 and HALO % SUBLANES == 0


def _dot(a, b):
    return jnp.dot(a, b, preferred_element_type=jnp.float32)


def _gelu_tanh(x):
    return x / (1.0 + jnp.exp2(x * (GELU_K1 + GELU_K3 * (x * x))))


def _normalise(y, gain, bias):
    mu = jnp.mean(y, axis=-1, keepdims=True)
    d = y - mu
    var = jnp.mean(d * d, axis=-1, keepdims=True)
    return d * lax.rsqrt(var + LN_EPS) * gain + bias


def _load_weight(src_hbm_ref, first_blk, n_blk, w_ref, stage_ref, sem_ref):
    ahead = N_WSTAGE - 1
    assert n_blk >= ahead

    def block_copy(c):
        c = jnp.asarray(c, jnp.int32)
        slot = lax.rem(c, N_WSTAGE)
        cols = pl.ds(pl.multiple_of(c * MXU_N, MXU_N), MXU_N)
        return pltpu.make_async_copy(src_hbm_ref.at[:, cols], stage_ref.at[slot], sem_ref.at[slot])

    for c in range(ahead):
        block_copy(c).start()

    def cast_block(c, carry):
        block_copy(c).wait()

        @pl.when(c + ahead < n_blk)
        def _():
            block_copy(c + ahead).start()

        w_ref[first_blk + c] = stage_ref[lax.rem(c, N_WSTAGE)].astype(jnp.bfloat16)
        return carry

    lax.fori_loop(0, n_blk, cast_block, 0)


def _body(tiles_per_seq, n_tiles,
          x_ref, w_in_hbm_ref, b_in_ref, cw_ref, cb_ref, gng_ref, gnb_ref, lvg_ref, lvb_ref,
          wsp_ref, bsp_ref, wpa_hbm_ref, wpb_hbm_ref, wo_hbm_ref, bo_ref, log_ref, lob_ref,
          o_ref,
          xb_ref, hbuf_ref, gate_ref, abuf_ref, ubuf_ref, vbuf_ref, vn_ref, sbuf_ref,
          mix_ref, wtril_ref, xprev_ref, xbprev_ref, w_ref, wstage_ref, wsem_ref):
    step = pl.program_id(0)
    tile = jnp.minimum(step, n_tiles - 1)

    @pl.when(step == 0)
    def _first_step():
        _load_weight(w_in_hbm_ref, 0, N_BLK_IN, w_ref, wstage_ref, wsem_ref)
        _load_weight(wpa_hbm_ref, BLK_W_PA, N_PAIR, w_ref, wstage_ref, wsem_ref)
        _load_weight(wpb_hbm_ref, BLK_W_PB, N_PAIR, w_ref, wstage_ref, wsem_ref)
        _load_weight(wo_hbm_ref, BLK_W_O, N_PAIR, w_ref, wstage_ref, wsem_ref)
        row = lax.broadcasted_iota(jnp.int32, (CHUNK, CHUNK), 0)
        col = lax.broadcasted_iota(jnp.int32, (CHUNK, CHUNK), 1)
        for h in range(N_HEADS):
            wtril_ref[h] = jnp.where(row >= col, wsp_ref[h], 0.0).astype(jnp.bfloat16)
        abuf_ref[...] = jnp.zeros(abuf_ref.shape, abuf_ref.dtype)
        sbuf_ref[...] = jnp.zeros(sbuf_ref.shape, sbuf_ref.dtype)
        xprev_ref[...] = jnp.zeros(xprev_ref.shape, xprev_ref.dtype)
        xbprev_ref[...] = jnp.zeros(xbprev_ref.shape, xbprev_ref.dtype)

    @pl.when(lax.rem(tile, tiles_per_seq) == 0)
    def _zero_conv_history():
        hbuf_ref[:, 0:HALO, :] = jnp.zeros((N_SLAB, HALO, LANES), jnp.float32)

    def proj(lhs_ref, blk, p):
        cols = slice(MXU_N * (blk + p), MXU_N * (blk + p + 1))
        return _dot(lhs_ref[...], w_ref[blk + p]) + b_in_ref[:, cols]

    def merge(p):
        cols = slice(MXU_N * p, MXU_N * (p + 1))
        ya = _dot(abuf_ref[...], w_ref[BLK_W_PA + p])
        yb = _dot(sbuf_ref[...], w_ref[BLK_W_PB + p])
        ga = jax.nn.sigmoid(proj(xbprev_ref, BLK_MERGE_A, p))
        gb = jax.nn.sigmoid(proj(xbprev_ref, BLK_MERGE_B, p))
        mix_ref[:, cols] = (ga * ya + gb * yb).astype(jnp.bfloat16)

    def out_proj():
        for p in range(N_PAIR):
            cols = slice(MXU_N * p, MXU_N * (p + 1))
            sub = _dot(mix_ref[...], w_ref[BLK_W_O + p]) + bo_ref[:, cols]
            o_ref[0, :, cols] = DEEPNORM_ALPHA * xprev_ref[:, cols] + sub
        o_ref[0] = _normalise(o_ref[0], log_ref[...], lob_ref[...])

    def branch_a_proj(p):
        cols = slice(MXU_N * p, MXU_N * (p + 1))
        h = proj(xb_ref, BLK_A_VAL, p) * jax.nn.sigmoid(proj(xb_ref, BLK_A_GLU, p))
        hbuf_ref[2 * p, HALO:HALO + TS, :] = h[:, :LANES]
        hbuf_ref[2 * p + 1, HALO:HALO + TS, :] = h[:, LANES:]
        gate_ref[:, cols] = jax.nn.silu(proj(xb_ref, BLK_A_GATE, p))

    def conv_slab(j):
        lanes = slice(LANES * j, LANES * (j + 1))
        wk = [jnp.broadcast_to(cw_ref[k:k + 1, lanes], (SUBLANES, LANES)) for k in range(CONV_K)]
        n_blocks = TS // SUBLANES
        first = HALO - (CONV_K - 1)
        accs = [None] * n_blocks
        for start in range(first, first + SUBLANES * (n_blocks - 1) + CONV_K):
            window = hbuf_ref[j, start:start + SUBLANES, :]
            for r in range(n_blocks):
                k = start - first - SUBLANES * r
                if 0 <= k < CONV_K:
                    t = wk[k] * window
                    accs[r] = t if accs[r] is None else accs[r] + t
        c = jnp.concatenate(accs, axis=0) + cb_ref[:, lanes]
        gn = _normalise(c, gng_ref[:, lanes], gnb_ref[:, lanes])
        abuf_ref[:, lanes] = (jax.nn.silu(gn) * gate_ref[:, lanes]).astype(jnp.bfloat16)

    def branch_b_proj():
        for p in range(N_PAIR):
            cols = slice(MXU_N * p, MXU_N * (p + 1))
            ubuf_ref[:, cols] = _gelu_tanh(proj(xb_ref, BLK_B_U, p))
            vbuf_ref[:, cols] = _gelu_tanh(proj(xb_ref, BLK_B_V, p))

    def branch_b_mix():
        vn_ref[...] = _normalise(vbuf_ref[...], lvg_ref[...], lvb_ref[...]).astype(jnp.bfloat16)
        for p in range(N_PAIR):
            bgate = jax.nn.silu(proj(xb_ref, BLK_B_GATE, p))
            for jj in range(2):
                hd = 2 * p + jj
                lanes = slice(LANES * hd, LANES * (hd + 1))
                bsp = bsp_ref[hd]
                for c0 in range(0, TS // CHUNK, 2):
                    rhs = jnp.concatenate([vn_ref[CHUNK * c0:CHUNK * (c0 + 1), lanes],
                                           vn_ref[CHUNK * (c0 + 1):CHUNK * (c0 + 2), lanes]], axis=1)
                    vmix = _dot(wtril_ref[hd], rhs)
                    for n in range(2):
                        rows = slice(CHUNK * (c0 + n), CHUNK * (c0 + n + 1))
                        vm = vmix[:, LANES * n:LANES * (n + 1)] + bsp
                        sblk = ubuf_ref[rows, lanes] * vm * bgate[rows, LANES * jj:LANES * (jj + 1)]
                        sbuf_ref[rows, lanes] = sblk.astype(jnp.bfloat16)

    xb_ref[...] = x_ref[0].astype(jnp.bfloat16)
    for p in range(N_PAIR):
        branch_a_proj(p)
    branch_b_proj()
    for p in range(N_PAIR):
        merge(p)
    out_proj()
    branch_b_mix()
    for j in range(N_SLAB):
        conv_slab(j)
    for j in range(N_SLAB):
        hbuf_ref[j, 0:HALO, :] = hbuf_ref[j, TS:TS + HALO, :]
    xprev_ref[...] = x_ref[0]
    xbprev_ref[...] = xb_ref[...]


def _const_spec(shape):
    zeros = (0,) * len(shape)
    return pl.BlockSpec(shape, lambda i: zeros, pipeline_mode=pl.Buffered(1))


def kernel(x, w_in, b_in, conv_w, conv_b, gn_g, gn_b, ln_v_g, ln_v_b, w_spatial, b_spatial, w_pa, w_pb, w_o, b_o, ln_out_g, ln_out_b):
    B, S, D = x.shape
    assert D == D_MODEL and S % TS == 0 and w_in.shape == (D_MODEL, N_BLK_IN * MXU_N)
    bf16 = jnp.bfloat16
    f32 = jnp.float32
    row = lambda v: v.reshape(1, -1).astype(f32)
    tiles_per_seq = S // TS
    n_tiles = B * tiles_per_seq

    weights = (w_in, w_pa, w_pb, w_o)
    assert all(w.dtype == f32 and w.shape[0] == D_MODEL for w in weights)
    operands = [
        x,
        w_in, row(b_in),
        conv_w.astype(f32), row(conv_b), row(gn_g), row(gn_b), row(ln_v_g), row(ln_v_b),
        w_spatial.astype(f32), b_spatial.astype(f32).reshape(N_HEADS, CHUNK, 1),
        w_pa, w_pb, w_o, row(b_o),
        row(ln_out_g), row(ln_out_b),
    ]

    def tile_block(t):
        return (t // tiles_per_seq, t % tiles_per_seq, 0)

    in_specs = [pl.BlockSpec((1, TS, D), lambda i: tile_block(jnp.minimum(i, n_tiles - 1)))]
    in_specs += [pl.BlockSpec(memory_space=pl.ANY) if any(op is w for w in weights) else _const_spec(op.shape)
                 for op in operands[1:]]

    scratch = [
        pltpu.VMEM((TS, D), bf16),
        pltpu.VMEM((N_SLAB, HALO + TS, LANES), f32),
        pltpu.VMEM((TS, D), f32),
        pltpu.VMEM((TS, D), bf16),
        pltpu.VMEM((TS, D), f32),
        pltpu.VMEM((TS, D), f32),
        pltpu.VMEM((TS, D), bf16),
        pltpu.VMEM((TS, D), bf16),
        pltpu.VMEM((TS, D), bf16),
        pltpu.VMEM((N_HEADS, CHUNK, CHUNK), bf16),
        pltpu.VMEM((TS, D), f32),
        pltpu.VMEM((TS, D), bf16),
        pltpu.VMEM((N_BLK_W, D_MODEL, MXU_N), bf16),
        pltpu.VMEM((N_WSTAGE, D_MODEL, MXU_N), f32),
        pltpu.SemaphoreType.DMA((N_WSTAGE,)),
    ]
    return pl.pallas_call(
        functools.partial(_body, tiles_per_seq, n_tiles),
        grid=(n_tiles + 1,),
        in_specs=in_specs,
        out_specs=pl.BlockSpec((1, TS, D), lambda i: tile_block(jnp.maximum(i - 1, 0))),
        out_shape=jax.ShapeDtypeStruct((B, S, D), x.dtype),
        scratch_shapes=scratch,
        compiler_params=pltpu.CompilerParams(
            dimension_semantics=("arbitrary",),
            vmem_limit_bytes=VMEM_LIMIT_BYTES,
        ),
        name="fused_hybrid_layer",
    )(*operands)
```

```python
import functools

import jax
import jax.numpy as jnp
from jax import lax
from jax.experimental import pallas as pl
from jax.experimental.pallas import tpu as pltpu

D_MODEL = 1024
CONV_K = 31
CHUNK = 128
N_HEADS = 8
N_GROUPS = 8
LANES = 128
SUBLANES = 8
MXU_N = 256
LN_EPS = 1e-5
DEEPNORM_ALPHA = 2.0 ** 0.25

N_PAIR = D_MODEL // MXU_N
BLK_A_VAL = 0 * N_PAIR
BLK_A_GLU = 1 * N_PAIR
BLK_A_GATE = 2 * N_PAIR
BLK_B_U = 3 * N_PAIR
BLK_B_V = 4 * N_PAIR
BLK_B_GATE = 5 * N_PAIR
BLK_MERGE_A = 6 * N_PAIR
BLK_MERGE_B = 7 * N_PAIR
N_BLK_IN = 8 * N_PAIR
BLK_W_PA = N_BLK_IN
BLK_W_PB = N_BLK_IN + N_PAIR
BLK_W_O = N_BLK_IN + 2 * N_PAIR
N_BLK_W = N_BLK_IN + 3 * N_PAIR
N_WSTAGE = 4
_LOG2_E = 1.4426950408889634
GELU_K1 = -2.0 * (2.0 / 3.141592653589793) ** 0.5 * _LOG2_E
GELU_K3 = GELU_K1 * 0.044715

TS = 512
HALO = 32
N_SLAB = D_MODEL // LANES
VMEM_LIMIT_BYTES = 58 * 1024 * 1024

assert D_MODEL // N_GROUPS == LANES and D_MODEL // N_HEADS == LANES
assert TS % (2 * CHUNK) == 0 and HALO >= CONV_K - 1 and HALO % SUBLANES == 0


def _dot(a, b):
    return jnp.dot(a, b, preferred_element_type=jnp.float32)


def _gelu_tanh(x):
    return x / (1.0 + jnp.exp2(x * (GELU_K1 + GELU_K3 * (x * x))))


def _normalise(y, gain, bias):
    mu = jnp.mean(y, axis=-1, keepdims=True)
    d = y - mu
    var = jnp.mean(d * d, axis=-1, keepdims=True)
    return d * lax.rsqrt(var + LN_EPS) * gain + bias


def _load_weight(src_hbm_ref, first_blk, n_blk, w_ref, stage_ref, sem_ref):
    ahead = N_WSTAGE - 1
    assert n_blk >= ahead

    def block_copy(c):
        c = jnp.asarray(c, jnp.int32)
        slot = lax.rem(c, N_WSTAGE)
        cols = pl.ds(pl.multiple_of(c * MXU_N, MXU_N), MXU_N)
        return pltpu.make_async_copy(src_hbm_ref.at[:, cols], stage_ref.at[slot], sem_ref.at[slot])

    for c in range(ahead):
        block_copy(c).start()

    def cast_block(c, carry):
        block_copy(c).wait()

        @pl.when(c + ahead < n_blk)
        def _():
            block_copy(c + ahead).start()

        w_ref[first_blk + c] = stage_ref[lax.rem(c, N_WSTAGE)].astype(jnp.bfloat16)
        return carry

    lax.fori_loop(0, n_blk, cast_block, 0)


def _body(tiles_per_seq, n_tiles,
          x_ref, xprev_ref, w_in_hbm_ref, b_in_ref, cw_ref, cb_ref, gng_ref, gnb_ref, lvg_ref, lvb_ref,
          wsp_ref, bsp_ref, wpa_hbm_ref, wpb_hbm_ref, wo_hbm_ref, bo_ref, log_ref, lob_ref,
          o_ref,
          xb_ref, hbuf_ref, gate_ref, abuf_ref, ubuf_ref, vbuf_ref, vn_ref, sbuf_ref,
          mix_ref, wtril_ref, xbprev_ref, w_ref, wstage_ref, wsem_ref):
    step = pl.program_id(0)
    tile = jnp.minimum(step, n_tiles - 1)

    @pl.when(step == 0)
    def _first_step():
        _load_weight(w_in_hbm_ref, 0, N_BLK_IN, w_ref, wstage_ref, wsem_ref)
        _load_weight(wpa_hbm_ref, BLK_W_PA, N_PAIR, w_ref, wstage_ref, wsem_ref)
        _load_weight(wpb_hbm_ref, BLK_W_PB, N_PAIR, w_ref, wstage_ref, wsem_ref)
        _load_weight(wo_hbm_ref, BLK_W_O, N_PAIR, w_ref, wstage_ref, wsem_ref)
        row = lax.broadcasted_iota(jnp.int32, (CHUNK, CHUNK), 0)
        col = lax.broadcasted_iota(jnp.int32, (CHUNK, CHUNK), 1)
        for h in range(N_HEADS):
            wtril_ref[h] = jnp.where(row >= col, wsp_ref[h], 0.0).astype(jnp.bfloat16)
        abuf_ref[...] = jnp.zeros(abuf_ref.shape, abuf_ref.dtype)
        sbuf_ref[...] = jnp.zeros(sbuf_ref.shape, sbuf_ref.dtype)
        xbprev_ref[...] = jnp.zeros(xbprev_ref.shape, xbprev_ref.dtype)

    @pl.when(lax.rem(tile, tiles_per_seq) == 0)
    def _zero_conv_history():
        hbuf_ref[:, 0:HALO, :] = jnp.zeros((N_SLAB, HALO, LANES), jnp.float32)

    def proj(lhs_ref, blk, p):
        cols = slice(MXU_N * (blk + p), MXU_N * (blk + p + 1))
        return _dot(lhs_ref[...], w_ref[blk + p]) + b_in_ref[:, cols]

    def merge(p):
        cols = slice(MXU_N * p, MXU_N * (p + 1))
        ya = _dot(abuf_ref[...], w_ref[BLK_W_PA + p])
        yb = _dot(sbuf_ref[...], w_ref[BLK_W_PB + p])
        ga = jax.nn.sigmoid(proj(xbprev_ref, BLK_MERGE_A, p))
        gb = jax.nn.sigmoid(proj(xbprev_ref, BLK_MERGE_B, p))
        mix_ref[:, cols] = (ga * ya + gb * yb).astype(jnp.bfloat16)

    def out_proj():
        for p in range(N_PAIR):
            cols = slice(MXU_N * p, MXU_N * (p + 1))
            sub = _dot(mix_ref[...], w_ref[BLK_W_O + p]) + bo_ref[:, cols]
            o_ref[0, :, cols] = DEEPNORM_ALPHA * xprev_ref[0, :, cols] + sub
        o_ref[0] = _normalise(o_ref[0], log_ref[...], lob_ref[...])

    def branch_a_proj(p):
        cols = slice(MXU_N * p, MXU_N * (p + 1))
        h = proj(xb_ref, BLK_A_VAL, p) * jax.nn.sigmoid(proj(xb_ref, BLK_A_GLU, p))
        hbuf_ref[2 * p, HALO:HALO + TS, :] = h[:, :LANES]
        hbuf_ref[2 * p + 1, HALO:HALO + TS, :] = h[:, LANES:]
        gate_ref[:, cols] = jax.nn.silu(proj(xb_ref, BLK_A_GATE, p))

    def conv_slab(j):
        lanes = slice(LANES * j, LANES * (j + 1))
        wk = [jnp.broadcast_to(cw_ref[k:k + 1, lanes], (SUBLANES, LANES)) for k in range(CONV_K)]
        accs = []
        for r in range(TS // SUBLANES):
            acc = None
            for k in range(CONV_K):
                start = SUBLANES * r + HALO - (CONV_K - 1) + k
                t = wk[k] * hbuf_ref[j, start:start + SUBLANES, :]
                acc = t if acc is None else acc + t
            accs.append(acc)
        c = jnp.concatenate(accs, axis=0) + cb_ref[:, lanes]
        gn = _normalise(c, gng_ref[:, lanes], gnb_ref[:, lanes])
        abuf_ref[:, lanes] = (jax.nn.silu(gn) * gate_ref[:, lanes]).astype(jnp.bfloat16)

    def branch_b():
        for p in range(N_PAIR):
            cols = slice(MXU_N * p, MXU_N * (p + 1))
            ubuf_ref[:, cols] = _gelu_tanh(proj(xb_ref, BLK_B_U, p))
            vbuf_ref[:, cols] = _gelu_tanh(proj(xb_ref, BLK_B_V, p))
        vn_ref[...] = _normalise(vbuf_ref[...], lvg_ref[...], lvb_ref[...]).astype(jnp.bfloat16)
        for p in range(N_PAIR):
            gate_ref[:, MXU_N * p:MXU_N * (p + 1)] = jax.nn.silu(proj(xb_ref, BLK_B_GATE, p))
            for jj in range(2):
                hd = 2 * p + jj
                lanes = slice(LANES * hd, LANES * (hd + 1))
                bsp = bsp_ref[hd]
                for c0 in range(0, TS // CHUNK, 2):
                    rhs = jnp.concatenate([vn_ref[CHUNK * c0:CHUNK * (c0 + 1), lanes],
                                           vn_ref[CHUNK * (c0 + 1):CHUNK * (c0 + 2), lanes]], axis=1)
                    vmix = _dot(wtril_ref[hd], rhs)
                    for n in range(2):
                        rows = slice(CHUNK * (c0 + n), CHUNK * (c0 + n + 1))
                        vm = vmix[:, LANES * n:LANES * (n + 1)] + bsp
                        sblk = ubuf_ref[rows, lanes] * vm * gate_ref[rows, lanes]
                        sbuf_ref[rows, lanes] = sblk.astype(jnp.bfloat16)

    xb_ref[...] = x_ref[0].astype(jnp.bfloat16)
    for p in range(N_PAIR):
        branch_a_proj(p)
    for p in range(N_PAIR):
        merge(p)
    out_proj()
    for j in range(N_SLAB):
        conv_slab(j)
    for j in range(N_SLAB):
        hbuf_ref[j, 0:HALO, :] = hbuf_ref[j, TS:TS + HALO, :]
    branch_b()
    xbprev_ref[...] = xb_ref[...]


def _const_spec(shape):
    zeros = (0,) * len(shape)
    return pl.BlockSpec(shape, lambda i: zeros, pipeline_mode=pl.Buffered(1))


def kernel(x, w_in, b_in, conv_w, conv_b, gn_g, gn_b, ln_v_g, ln_v_b, w_spatial, b_spatial, w_pa, w_pb, w_o, b_o, ln_out_g, ln_out_b):
    B, S, D = x.shape
    assert D == D_MODEL and S % TS == 0 and w_in.shape == (D_MODEL, N_BLK_IN * MXU_N)
    bf16 = jnp.bfloat16
    f32 = jnp.float32
    row = lambda v: v.reshape(1, -1).astype(f32)
    tiles_per_seq = S // TS
    n_tiles = B * tiles_per_seq

    weights = (w_in, w_pa, w_pb, w_o)
    assert all(w.dtype == f32 and w.shape[0] == D_MODEL for w in weights)
    operands = [
        x, x,
        w_in, row(b_in),
        conv_w.astype(f32), row(conv_b), row(gn_g), row(gn_b), row(ln_v_g), row(ln_v_b),
        w_spatial.astype(f32), b_spatial.astype(f32).reshape(N_HEADS, CHUNK, 1),
        w_pa, w_pb, w_o, row(b_o),
        row(ln_out_g), row(ln_out_b),
    ]

    def tile_block(t):
        return (t // tiles_per_seq, t % tiles_per_seq, 0)

    in_specs = [pl.BlockSpec((1, TS, D), lambda i: tile_block(jnp.minimum(i, n_tiles - 1))),
                pl.BlockSpec((1, TS, D), lambda i: tile_block(jnp.maximum(i - 1, 0)))]
    in_specs += [pl.BlockSpec(memory_space=pl.ANY) if any(op is w for w in weights) else _const_spec(op.shape)
                 for op in operands[2:]]

    scratch = [
        pltpu.VMEM((TS, D), bf16),
        pltpu.VMEM((N_SLAB, HALO + TS, LANES), f32),
        pltpu.VMEM((TS, D), f32),
        pltpu.VMEM((TS, D), bf16),
        pltpu.VMEM((TS, D), f32),
        pltpu.VMEM((TS, D), f32),
        pltpu.VMEM((TS, D), bf16),
        pltpu.VMEM((TS, D), bf16),
        pltpu.VMEM((TS, D), bf16),
        pltpu.VMEM((N_HEADS, CHUNK, CHUNK), bf16),
        pltpu.VMEM((TS, D), bf16),
        pltpu.VMEM((N_BLK_W, D_MODEL, MXU_N), bf16),
        pltpu.VMEM((N_WSTAGE, D_MODEL, MXU_N), f32),
        pltpu.SemaphoreType.DMA((N_WSTAGE,)),
    ]
    return pl.pallas_call(
        functools.partial(_body, tiles_per_seq, n_tiles),
        grid=(n_tiles + 1,),
        in_specs=in_specs,
        out_specs=pl.BlockSpec((1, TS, D), lambda i: tile_block(jnp.maximum(i - 1, 0))),
        out_shape=jax.ShapeDtypeStruct((B, S, D), x.dtype),
        scratch_shapes=scratch,
        compiler_params=pltpu.CompilerParams(
            dimension_semantics=("arbitrary",),
            vmem_limit_bytes=VMEM_LIMIT_BYTES,
        ),
        name="fused_hybrid_layer",
    )(*operands)
```

```python
import functools

import jax
import jax.numpy as jnp
from jax import lax
from jax.experimental import pallas as pl
from jax.experimental.pallas import tpu as pltpu

D_MODEL = 1024
CONV_K = 31
CHUNK = 128
N_HEADS = 8
N_GROUPS = 8
LANES = 128
SUBLANES = 8
MXU_N = 256
LN_EPS = 1e-5
DEEPNORM_ALPHA = 2.0 ** 0.25

N_PAIR = D_MODEL // MXU_N
BLK_A_VAL = 0 * N_PAIR
BLK_A_GLU = 1 * N_PAIR
BLK_A_GATE = 2 * N_PAIR
BLK_B_U = 3 * N_PAIR
BLK_B_V = 4 * N_PAIR
BLK_B_GATE = 5 * N_PAIR
BLK_MERGE_A = 6 * N_PAIR
BLK_MERGE_B = 7 * N_PAIR
N_BLK_IN = 8 * N_PAIR
BLK_W_PA = N_BLK_IN
BLK_W_PB = N_BLK_IN + N_PAIR
BLK_W_O = N_BLK_IN + 2 * N_PAIR
N_BLK_W = N_BLK_IN + 3 * N_PAIR
N_WSTAGE = 4
_LOG2_E = 1.4426950408889634
GELU_K1 = -2.0 * (2.0 / 3.141592653589793) ** 0.5 * _LOG2_E
GELU_K3 = GELU_K1 * 0.044715

TS = 512
HALO = 32
N_SLAB = D_MODEL // LANES
VMEM_LIMIT_BYTES = 58 * 1024 * 1024

assert D_MODEL // N_GROUPS == LANES and D_MODEL // N_HEADS == LANES
assert TS % (2 * CHUNK) == 0 and HALO >= CONV_K - 1 and HALO % SUBLANES == 0


def _dot(a, b):
    return jnp.dot(a, b, preferred_element_type=jnp.float32)


def _gelu_tanh(x):
    return x / (1.0 + jnp.exp2(x * (GELU_K1 + GELU_K3 * (x * x))))


def _normalise(y, gain, bias):
    mu = jnp.mean(y, axis=-1, keepdims=True)
    d = y - mu
    var = jnp.mean(d * d, axis=-1, keepdims=True)
    return d * lax.rsqrt(var + LN_EPS) * gain + bias


def _load_weight(src_hbm_ref, first_blk, n_blk, w_ref, stage_ref, sem_ref):
    ahead = N_WSTAGE - 1
    assert n_blk >= ahead

    def block_copy(c):
        c = jnp.asarray(c, jnp.int32)
        slot = lax.rem(c, N_WSTAGE)
        cols = pl.ds(pl.multiple_of(c * MXU_N, MXU_N), MXU_N)
        return pltpu.make_async_copy(src_hbm_ref.at[:, cols], stage_ref.at[slot], sem_ref.at[slot])

    for c in range(ahead):
        block_copy(c).start()

    def cast_block(c, carry):
        block_copy(c).wait()

        @pl.when(c + ahead < n_blk)
        def _():
            block_copy(c + ahead).start()

        w_ref[first_blk + c] = stage_ref[lax.rem(c, N_WSTAGE)].astype(jnp.bfloat16)
        return carry

    lax.fori_loop(0, n_blk, cast_block, 0)


def _body(tiles_per_seq, n_tiles,
          x_ref, xprev_ref, w_in_hbm_ref, b_in_ref, cw_ref, cb_ref, gng_ref, gnb_ref, lvg_ref, lvb_ref,
          wsp_ref, bsp_ref, wpa_hbm_ref, wpb_hbm_ref, wo_hbm_ref, bo_ref, log_ref, lob_ref,
          o_ref,
          xb_ref, hbuf_ref, gate_ref, abuf_ref, ubuf_ref, vbuf_ref, vn_ref, sbuf_ref,
          mix_ref, wtril_ref, xbprev_ref, w_ref, wstage_ref, wsem_ref):
    step = pl.program_id(0)
    tile = jnp.minimum(step, n_tiles - 1)

    @pl.when(step == 0)
    def _first_step():
        _load_weight(w_in_hbm_ref, 0, N_BLK_IN, w_ref, wstage_ref, wsem_ref)
        _load_weight(wpa_hbm_ref, BLK_W_PA, N_PAIR, w_ref, wstage_ref, wsem_ref)
        _load_weight(wpb_hbm_ref, BLK_W_PB, N_PAIR, w_ref, wstage_ref, wsem_ref)
        _load_weight(wo_hbm_ref, BLK_W_O, N_PAIR, w_ref, wstage_ref, wsem_ref)
        row = lax.broadcasted_iota(jnp.int32, (CHUNK, CHUNK), 0)
        col = lax.broadcasted_iota(jnp.int32, (CHUNK, CHUNK), 1)
        for h in range(N_HEADS):
            wtril_ref[h] = jnp.where(row >= col, wsp_ref[h], 0.0).astype(jnp.bfloat16)
        abuf_ref[...] = jnp.zeros(abuf_ref.shape, abuf_ref.dtype)
        sbuf_ref[...] = jnp.zeros(sbuf_ref.shape, sbuf_ref.dtype)
        xbprev_ref[...] = jnp.zeros(xbprev_ref.shape, xbprev_ref.dtype)

    @pl.when(lax.rem(tile, tiles_per_seq) == 0)
    def _zero_conv_history():
        hbuf_ref[:, 0:HALO, :] = jnp.zeros((N_SLAB, HALO, LANES), jnp.float32)

    def proj(lhs_ref, blk, p):
        cols = slice(MXU_N * (blk + p), MXU_N * (blk + p + 1))
        return _dot(lhs_ref[...], w_ref[blk + p]) + b_in_ref[:, cols]

    def merge(p):
        cols = slice(MXU_N * p, MXU_N * (p + 1))
        ya = _dot(abuf_ref[...], w_ref[BLK_W_PA + p])
        yb = _dot(sbuf_ref[...], w_ref[BLK_W_PB + p])
        ga = jax.nn.sigmoid(proj(xbprev_ref, BLK_MERGE_A, p))
        gb = jax.nn.sigmoid(proj(xbprev_ref, BLK_MERGE_B, p))
        mix_ref[:, cols] = (ga * ya + gb * yb).astype(jnp.bfloat16)

    def out_proj():
        for p in range(N_PAIR):
            cols = slice(MXU_N * p, MXU_N * (p + 1))
            sub = _dot(mix_ref[...], w_ref[BLK_W_O + p]) + bo_ref[:, cols]
            o_ref[0, :, cols] = DEEPNORM_ALPHA * xprev_ref[0, :, cols] + sub
        o_ref[0] = _normalise(o_ref[0], log_ref[...], lob_ref[...])

    def branch_a_proj(p):
        cols = slice(MXU_N * p, MXU_N * (p + 1))
        h = proj(xb_ref, BLK_A_VAL, p) * jax.nn.sigmoid(proj(xb_ref, BLK_A_GLU, p))
        hbuf_ref[2 * p, HALO:HALO + TS, :] = h[:, :LANES]
        hbuf_ref[2 * p + 1, HALO:HALO + TS, :] = h[:, LANES:]
        gate_ref[:, cols] = jax.nn.silu(proj(xb_ref, BLK_A_GATE, p))

    def conv_slab(j):
        lanes = slice(LANES * j, LANES * (j + 1))
        wk = [jnp.broadcast_to(cw_ref[k:k + 1, lanes], (SUBLANES, LANES)) for k in range(CONV_K)]
        accs = []
        for r in range(TS // SUBLANES):
            acc = None
            for k in range(CONV_K):
                start = SUBLANES * r + HALO - (CONV_K - 1) + k
                t = wk[k] * hbuf_ref[j, start:start + SUBLANES, :]
                acc = t if acc is None else acc + t
            accs.append(acc)
        c = jnp.concatenate(accs, axis=0) + cb_ref[:, lanes]
        gn = _normalise(c, gng_ref[:, lanes], gnb_ref[:, lanes])
        abuf_ref[:, lanes] = (jax.nn.silu(gn) * gate_ref[:, lanes]).astype(jnp.bfloat16)

    def branch_b():
        for p in range(N_PAIR):
            cols = slice(MXU_N * p, MXU_N * (p + 1))
            ubuf_ref[:, cols] = _gelu_tanh(proj(xb_ref, BLK_B_U, p))
            vbuf_ref[:, cols] = _gelu_tanh(proj(xb_ref, BLK_B_V, p))
        vn_ref[...] = _normalise(vbuf_ref[...], lvg_ref[...], lvb_ref[...]).astype(jnp.bfloat16)
        for p in range(N_PAIR):
            bgate = jax.nn.silu(proj(xb_ref, BLK_B_GATE, p))
            for jj in range(2):
                hd = 2 * p + jj
                lanes = slice(LANES * hd, LANES * (hd + 1))
                bsp = bsp_ref[hd]
                for c0 in range(0, TS // CHUNK, 2):
                    rhs = jnp.concatenate([vn_ref[CHUNK * c0:CHUNK * (c0 + 1), lanes],
                                           vn_ref[CHUNK * (c0 + 1):CHUNK * (c0 + 2), lanes]], axis=1)
                    vmix = _dot(wtril_ref[hd], rhs)
                    for n in range(2):
                        rows = slice(CHUNK * (c0 + n), CHUNK * (c0 + n + 1))
                        vm = vmix[:, LANES * n:LANES * (n + 1)] + bsp
                        sblk = ubuf_ref[rows, lanes] * vm * bgate[rows, LANES * jj:LANES * (jj + 1)]
                        sbuf_ref[rows, lanes] = sblk.astype(jnp.bfloat16)

    xb_ref[...] = x_ref[0].astype(jnp.bfloat16)
    for p in range(N_PAIR):
        branch_a_proj(p)
    for p in range(N_PAIR):
        merge(p)
    out_proj()
    for j in range(N_SLAB):
        conv_slab(j)
    for j in range(N_SLAB):
        hbuf_ref[j, 0:HALO, :] = hbuf_ref[j, TS:TS + HALO, :]
    branch_b()
    xbprev_ref[...] = xb_ref[...]


def _const_spec(shape):
    zeros = (0,) * len(shape)
    return pl.BlockSpec(shape, lambda i: zeros, pipeline_mode=pl.Buffered(1))


def kernel(x, w_in, b_in, conv_w, conv_b, gn_g, gn_b, ln_v_g, ln_v_b, w_spatial, b_spatial, w_pa, w_pb, w_o, b_o, ln_out_g, ln_out_b):
    B, S, D = x.shape
    assert D == D_MODEL and S % TS == 0 and w_in.shape == (D_MODEL, N_BLK_IN * MXU_N)
    bf16 = jnp.bfloat16
    f32 = jnp.float32
    row = lambda v: v.reshape(1, -1).astype(f32)
    tiles_per_seq = S // TS
    n_tiles = B * tiles_per_seq

    weights = (w_in, w_pa, w_pb, w_o)
    assert all(w.dtype == f32 and w.shape[0] == D_MODEL for w in weights)
    operands = [
        x, x,
        w_in, row(b_in),
        conv_w.astype(f32), row(conv_b), row(gn_g), row(gn_b), row(ln_v_g), row(ln_v_b),
        w_spatial.astype(f32), b_spatial.astype(f32).reshape(N_HEADS, CHUNK, 1),
        w_pa, w_pb, w_o, row(b_o),
        row(ln_out_g), row(ln_out_b),
    ]

    def tile_block(t):
        return (t // tiles_per_seq, t % tiles_per_seq, 0)

    in_specs = [pl.BlockSpec((1, TS, D), lambda i: tile_block(jnp.minimum(i, n_tiles - 1))),
                pl.BlockSpec((1, TS, D), lambda i: tile_block(jnp.maximum(i - 1, 0)))]
    in_specs += [pl.BlockSpec(memory_space=pl.ANY) if any(op is w for w in weights) else _const_spec(op.shape)
                 for op in operands[2:]]

    scratch = [
        pltpu.VMEM((TS, D), bf16),
        pltpu.VMEM((N_SLAB, HALO + TS, LANES), f32),
        pltpu.VMEM((TS, D), f32),
        pltpu.VMEM((TS, D), bf16),
        pltpu.VMEM((TS, D), f32),
        pltpu.VMEM((TS, D), f32),
        pltpu.VMEM((TS, D), bf16),
        pltpu.VMEM((TS, D), bf16),
        pltpu.VMEM((TS, D), bf16),
        pltpu.VMEM((N_HEADS, CHUNK, CHUNK), bf16),
        pltpu.VMEM((TS, D), bf16),
        pltpu.VMEM((N_BLK_W, D_MODEL, MXU_N), bf16),
        pltpu.VMEM((N_WSTAGE, D_MODEL, MXU_N), f32),
        pltpu.SemaphoreType.DMA((N_WSTAGE,)),
    ]
    return pl.pallas_call(
        functools.partial(_body, tiles_per_seq, n_tiles),
        grid=(n_tiles + 1,),
        in_specs=in_specs,
        out_specs=pl.BlockSpec((1, TS, D), lambda i: tile_block(jnp.maximum(i - 1, 0))),
        out_shape=jax.ShapeDtypeStruct((B, S, D), x.dtype),
        scratch_shapes=scratch,
        compiler_params=pltpu.CompilerParams(
            dimension_semantics=("arbitrary",),
            vmem_limit_bytes=VMEM_LIMIT_BYTES,
        ),
        name="fused_hybrid_layer",
    )(*operands)
```
